```python
import jax
import jax.numpy as jnp
from jax import lax
import numpy as np

D_MODEL = 1024
BATCH = 8
SEQ = 2048
DEPTH = 2
DEC_BATCH = 128
DEC_SEQ = 4
PAST_LEN = 2048
PAGE_SIZE = 128

HEAD_DIM = 64
H_RWKV = 5
H_RET = 5
H_DIL = 6
C_RWKV = H_RWKV * HEAD_DIM
C_RET = H_RET * HEAD_DIM
C_DIL = H_DIL * HEAD_DIM
MIX_WIDTH = C_RWKV + C_RET + C_DIL
IN_COLS = 3 * C_RWKV + 4 * C_RET + 3 * C_DIL
D_DECAY_LORA = 64
D_AAA_LORA = 64
D_GATE_LORA = 128
RWKV_GN_EPS = 64e-5
RET_CHUNK = 128
RET_GN_EPS = 1e-5
DILATED_PATTERNS = ((128, 1), (512, 4), (2048, 16))
MAX_WINDOW = 2048
ROPE_THETA = 10000.0
N_EXPERTS = 64
D_EXPERT = 128
TOP_K = 8
N_GROUPS = 8
TOPK_GROUPS = 4
ROUTE_SCALE = 2.5
LN_EPS = 1e-5
DEEPNORM_ALPHA = (2 * DEPTH) ** 0.25
DEEPNORM_BETA = (8 * DEPTH) ** -0.25

kernel_name = 'hybrid_rwkv7_retnet_dilated_moe_step'


def layer_norm(x, g, b):
    xf = x.astype(jnp.float32)
    mu = jnp.mean(xf, -1, keepdims=True)
    var = jnp.mean(jnp.square(xf - mu), -1, keepdims=True)
    return ((xf - mu) * lax.rsqrt(var + LN_EPS)).astype(x.dtype) * g + b


def head_norm(y, eps):
    yf = y.astype(jnp.float32)
    mu = jnp.mean(yf, -1, keepdims=True)
    var = jnp.mean(jnp.square(yf - mu), -1, keepdims=True)
    return (yf - mu) * lax.rsqrt(var + eps)


def split_heads(t, n_heads):
    return t.reshape(t.shape[0], t.shape[1], n_heads, HEAD_DIM)


def apply_rotary(x, pos, inv_freq):
    ang = pos.astype(jnp.float32)[:, None] * inv_freq[None, :]
    cos = jnp.cos(ang)[None, :, None, :].astype(x.dtype)
    sin = jnp.sin(ang)[None, :, None, :].astype(x.dtype)
    x1, x2 = jnp.split(x, 2, axis=-1)
    return jnp.concatenate([x1 * cos - x2 * sin, x2 * cos + x1 * sin], axis=-1)


def rwkv7_mixer(x, xx, rkv, rkv_prev, S0, mu_rkv, mu_wag, w0, w1, w2, a0, a1, a2,
                g1, g2, k_k, k_a, r_k, lnx_g, lnx_b):
    B, T, _ = x.shape
    f32 = jnp.float32
    rkv = rkv + mu_rkv.reshape(-1) * (rkv_prev - rkv)
    r, k, v = jnp.split(rkv, 3, axis=-1)
    xw = x + xx * mu_wag[0]
    xa = x + xx * mu_wag[1]
    xg = x + xx * mu_wag[2]
    w_log = -jax.nn.softplus(-(w0 + jnp.tanh(xw @ w1) @ w2)) - 0.5
    decay = jnp.exp(-jnp.exp(w_log.astype(f32)))
    a = jax.nn.sigmoid(a0 + (xa @ a1) @ a2)
    g = jax.nn.sigmoid(xg @ g1) @ g2
    kk = split_heads(k * k_k, H_RWKV).astype(f32)
    kk = kk * lax.rsqrt(jnp.maximum(jnp.sum(kk * kk, -1, keepdims=True), 1e-24))
    k = k * (1 + (a - 1) * k_a)
    r_h, k_h, v_h = split_heads(r, H_RWKV), split_heads(k, H_RWKV), split_heads(v, H_RWKV)
    w_h, a_h = split_heads(decay, H_RWKV), split_heads(a, H_RWKV)

    def step(S, inp):
        r_t, w_t, k_t, v_t, kk_t, a_t = inp
        sa = jnp.einsum('bhvk,bhk->bhv', S, kk_t)
        S = (S * w_t[:, :, None, :] - sa[..., None] * (kk_t * a_t)[:, :, None, :]
             + v_t[..., None] * k_t[:, :, None, :])
        return S, jnp.einsum('bhvk,bhk->bhv', S, r_t)

    def time_major(t):
        return jnp.moveaxis(t.astype(f32), 1, 0)

    S, ys = lax.scan(step, S0.astype(f32),
                     (time_major(r_h), time_major(w_h), time_major(k_h), time_major(v_h),
                      time_major(kk), time_major(a_h)))
    y = head_norm(jnp.moveaxis(ys, 0, 1), RWKV_GN_EPS).astype(x.dtype).reshape(B, T, C_RWKV)
    y = y * lnx_g + lnx_b
    bonus = jnp.sum(r_h * k_h * r_k, -1, keepdims=True) * v_h
    y = y + bonus.reshape(B, T, C_RWKV)
    return y * g, S


def retention_chunk(S, q, k, v, log_gamma):
    C = q.shape[1]
    i = jnp.arange(C, dtype=jnp.float32)
    rel = i[:, None] - i[None, :]
    dmat = jnp.where(rel >= 0, jnp.exp(log_gamma[:, None, None] * jnp.maximum(rel, 0.0)), 0.0)
    scores = jnp.einsum('bihn,bjhn->bhij', q, k) * dmat
    o = jnp.einsum('bhij,bjhv->bihv', scores, v)
    cross_decay = jnp.exp(log_gamma[None, :] * (i[:, None] + 1.0))
    o = o + jnp.einsum('bihn,bhnv->bihv', q, S) * cross_decay[None, :, :, None]
    in_decay = jnp.exp(log_gamma[None, :] * (C - 1.0 - i)[:, None])
    S = (S * jnp.exp(log_gamma * C)[None, :, None, None]
         + jnp.einsum('bjhn,bjhv->bhnv', k * in_decay[None, :, :, None], v))
    return S, o


def retention_mixer(q, k, v, g, pos, S0):
    B, T, _ = q.shape
    f32 = jnp.float32
    inv_freq = 1.0 / (10000.0 ** jnp.linspace(0.0, 1.0, HEAD_DIM // 2, dtype=f32))
    q = apply_rotary(split_heads(q, H_RET), pos, inv_freq).astype(f32)
    k = (apply_rotary(split_heads(k, H_RET), pos, inv_freq) * HEAD_DIM ** -0.5).astype(f32)
    v = split_heads(v, H_RET).astype(f32)
    log_gamma = jnp.log(1.0 - 2.0 ** (-5.0 - jnp.arange(H_RET, dtype=f32)))
    chunk = RET_CHUNK if T % RET_CHUNK == 0 else T
    nc = T // chunk

    def to_chunks(t):
        return jnp.moveaxis(t.reshape(B, nc, chunk, H_RET, HEAD_DIM), 1, 0)

    S, o = lax.scan(lambda S, c: retention_chunk(S, c[0], c[1], c[2], log_gamma),
                    S0.astype(f32), (to_chunks(q), to_chunks(k), to_chunks(v)))
    o = jnp.moveaxis(o, 0, 1).reshape(B, T, H_RET, HEAD_DIM)
    y = head_norm(o, RET_GN_EPS).astype(g.dtype).reshape(B, T, C_RET)
    return jax.nn.silu(g) * y, S


def dilated_attention_prompt(q, k, v):
    B, T, H, N = q.shape
    outs, lses = [], []
    for window, dil in DILATED_PATTERNS:
        blk = window // dil
        L = T // dil
        nb = -(-L // blk)
        Lp = nb * blk

        def to_blocks(t):
            t = t.reshape(B, L, dil, H, N).transpose(0, 2, 1, 3, 4)
            t = jnp.pad(t, ((0, 0), (0, 0), (0, Lp - L), (0, 0), (0, 0)))
            return t.reshape(B, dil, nb, blk, H, N)

        def with_prev_block(t):
            prev = jnp.pad(t, ((0, 0), (0, 0), (1, 0), (0, 0), (0, 0), (0, 0)))[:, :, :nb]
            return jnp.concatenate([prev, t], axis=3)

        qb = to_blocks(q)
        kb = with_prev_block(to_blocks(k))
        vb = with_prev_block(to_blocks(v))
        s = jnp.einsum('bdcihn,bdcjhn->bdchij', qb, kb, preferred_element_type=jnp.float32)
        ii = jnp.arange(blk)[:, None]
        jj = jnp.arange(2 * blk)[None, :]
        band = (jj >= ii) & (jj <= ii + blk)
        has_prev = (jnp.arange(nb) > 0)[:, None, None] | (jj >= blk)[None]
        valid = band[None] & has_prev
        s = jnp.where(valid[None, None, :, None], s, -jnp.inf)
        lse = jax.nn.logsumexp(s, axis=-1)
        p = jnp.exp(s - lse[..., None]).astype(v.dtype)
        o = jnp.einsum('bdchij,bdcjhn->bdcihn', p, vb)
        o = o.reshape(B, dil, Lp, H, N)[:, :, :L].transpose(0, 2, 1, 3, 4).reshape(B, T, H, N)
        lse = lse.transpose(0, 1, 2, 4, 3).reshape(B, dil, Lp, H)[:, :, :L]
        lse = lse.transpose(0, 2, 1, 3).reshape(B, T, H)
        outs.append(o)
        lses.append(lse)
    return jnp.stack(outs), jnp.stack(lses)


def dilated_attention_sample(q, k_all, v_all):
    B, n, H, N = q.shape
    n_buf = k_all.shape[1] - n
    outs, lses = [], []
    for window, dil in DILATED_PATTERNS:
        n_keys = window // dil + 1
        idx = n_buf + jnp.arange(n)[:, None] - dil * jnp.arange(n_keys)[None, :]
        valid = idx >= 0
        flat = jnp.maximum(idx, 0).reshape(-1)
        kg = jnp.take(k_all, flat, axis=1).reshape(B, n, n_keys, H, N)
        vg = jnp.take(v_all, flat, axis=1).reshape(B, n, n_keys, H, N)
        s = jnp.einsum('bihn,bijhn->bihj', q, kg, preferred_element_type=jnp.float32)
        s = jnp.where(valid[None, :, None, :], s, -jnp.inf)
        lse = jax.nn.logsumexp(s, axis=-1)
        p = jnp.exp(s - lse[..., None]).astype(v_all.dtype)
        outs.append(jnp.einsum('bihj,bijhn->bihn', p, vg))
        lses.append(lse)
    return jnp.stack(outs), jnp.stack(lses)


def dilated_mixer(q, k, v, pos, win_k, win_v):
    B, T, _ = q.shape
    inv_freq = ROPE_THETA ** (-jnp.arange(0, HEAD_DIM, 2, dtype=jnp.float32) / HEAD_DIM)
    q = apply_rotary(split_heads(q, H_DIL), pos, inv_freq) * HEAD_DIM ** -0.5
    k = apply_rotary(split_heads(k, H_DIL), pos, inv_freq)
    v = split_heads(v, H_DIL)
    if win_k is None:
        outs, lses = dilated_attention_prompt(q, k, v)
        keep = min(MAX_WINDOW, T)
        new_k, new_v = k[:, T - keep:], v[:, T - keep:]
    else:
        k_all = jnp.concatenate([win_k.astype(k.dtype), k], axis=1)
        v_all = jnp.concatenate([win_v.astype(v.dtype), v], axis=1)
        outs, lses = dilated_attention_sample(q, k_all, v_all)
        keep = win_k.shape[1]
        new_k, new_v = k_all[:, -keep:], v_all[:, -keep:]
    weights = jax.nn.softmax(lses, axis=0)
    o = jnp.einsum('pbth,pbthn->bthn', weights, outs.astype(jnp.float32)).astype(q.dtype)
    return o.reshape(B, T, C_DIL), new_k, new_v


def mixer_block(x, pos, x_last, rwkv_S, ret_S, win_k, win_v, w_in, mu_rkv, mu_wag, w0, w1, w2,
                a0, a1, a2, g1, g2, k_k, k_a, r_k, lnx_g, lnx_b, w_out):
    proj = x @ w_in
    pa, pb, pc = jnp.split(proj, [3 * C_RWKV, 3 * C_RWKV + 4 * C_RET], axis=-1)
    x_prev = jnp.concatenate([x_last[:, None], x[:, :-1]], axis=1)
    pa_last = x_last @ w_in[:, :3 * C_RWKV]
    pa_prev = jnp.concatenate([pa_last[:, None], pa[:, :-1]], axis=1)
    y_a, rwkv_S = rwkv7_mixer(x, x_prev - x, pa, pa_prev, rwkv_S, mu_rkv, mu_wag, w0, w1, w2,
                              a0, a1, a2, g1, g2, k_k, k_a, r_k, lnx_g, lnx_b)
    q_b, k_b, v_b, g_b = jnp.split(pb, 4, axis=-1)
    y_b, ret_S = retention_mixer(q_b, k_b, v_b, g_b, pos, ret_S)
    q_c, k_c, v_c = jnp.split(pc, 3, axis=-1)
    y_c, new_k, new_v = dilated_mixer(q_c, k_c, v_c, pos, win_k, win_v)
    y = jnp.concatenate([y_a, y_b, y_c], axis=-1) @ w_out
    return y, rwkv_S, x[:, -1], ret_S, new_k, new_v


def moe_block(x, w_router, router_bias, w_gate, w_up, w_down, ws_gate, ws_up, ws_down):
    B, T, D = x.shape
    xt = x.reshape(B * T, D)
    scores = jax.nn.sigmoid((xt @ w_router).astype(jnp.float32))
    biased = scores + router_bias.astype(jnp.float32)
    grouped = biased.reshape(-1, N_GROUPS, N_EXPERTS // N_GROUPS)
    group_score = jnp.sum(lax.top_k(grouped, 2)[0], axis=-1)
    _, top_groups = lax.top_k(group_score, TOPK_GROUPS)
    group_mask = jnp.sum(jax.nn.one_hot(top_groups, N_GROUPS), axis=1) > 0
    masked = jnp.where(group_mask[:, :, None], grouped, -jnp.inf).reshape(-1, N_EXPERTS)
    _, top_idx = lax.top_k(masked, TOP_K)
    top_s = jnp.take_along_axis(scores, top_idx, axis=-1)
    gates = top_s / jnp.sum(top_s, -1, keepdims=True) * ROUTE_SCALE
    combine = jnp.sum(jax.nn.one_hot(top_idx, N_EXPERTS, dtype=jnp.float32) * gates[..., None], axis=1)
    h = jax.nn.silu(jnp.einsum('td,edf->tef', xt, w_gate)) * jnp.einsum('td,edf->tef', xt, w_up)
    routed = jnp.einsum('tef,efd->td', h * combine[..., None].astype(h.dtype), w_down)
    shared = (jax.nn.silu(xt @ ws_gate) * (xt @ ws_up)) @ ws_down
    return (routed + shared).reshape(B, T, D)


def setup_inputs(seed: int = 0) -> dict:
    key = jax.random.key(seed)
    keys = jax.random.split(key, 40)
    f32 = jnp.float32

    def nrm(i, shape, scale=1.0):
        return jax.random.normal(keys[i], shape, f32) * scale

    def uni(i, shape):
        return jax.random.uniform(keys[i], shape, f32)

    win_buf = min(MAX_WINDOW, PAST_LEN)
    L = DEPTH
    return {
        'x_prompt': nrm(0, (BATCH, SEQ, D_MODEL)),
        'x_sample': nrm(1, (DEC_BATCH, DEC_SEQ, D_MODEL)),
        'state_rwkv': nrm(2, (L, DEC_BATCH, H_RWKV, HEAD_DIM, HEAD_DIM), 0.5),
        'state_shift': nrm(3, (L, DEC_BATCH, D_MODEL)),
        'state_ret': nrm(4, (L, DEC_BATCH, H_RET, HEAD_DIM, HEAD_DIM), 0.5),
        'cache_k_win': nrm(5, (L, DEC_BATCH, win_buf, H_DIL, HEAD_DIM)),
        'cache_v_win': nrm(6, (L, DEC_BATCH, win_buf, H_DIL, HEAD_DIM)),
        'w_in': nrm(7, (L, D_MODEL, IN_COLS), D_MODEL ** -0.5),
        'mu_rkv': uni(8, (L, 3, C_RWKV)),
        'mu_wag': uni(9, (L, 3, D_MODEL)),
        'w0': nrm(10, (L, C_RWKV), 0.5),
        'w1': nrm(11, (L, D_MODEL, D_DECAY_LORA), D_MODEL ** -0.5),
        'w2': nrm(12, (L, D_DECAY_LORA, C_RWKV), D_DECAY_LORA ** -0.5),
        'a0': nrm(13, (L, C_RWKV), 0.5),
        'a1': nrm(14, (L, D_MODEL, D_AAA_LORA), D_MODEL ** -0.5),
        'a2': nrm(15, (L, D_AAA_LORA, C_RWKV), D_AAA_LORA ** -0.5),
        'g1': nrm(16, (L, D_MODEL, D_GATE_LORA), D_MODEL ** -0.5),
        'g2': nrm(17, (L, D_GATE_LORA, C_RWKV), D_GATE_LORA ** -0.5),
        'k_k': 0.85 + nrm(18, (L, C_RWKV), 0.05),
        'k_a': 1.0 + nrm(19, (L, C_RWKV), 0.05),
        'r_k': nrm(20, (L, H_RWKV, HEAD_DIM), 0.1),
        'lnx_g': 1.0 + nrm(21, (L, C_RWKV), 0.02),
        'lnx_b': nrm(22, (L, C_RWKV), 0.02),
        'w_out': nrm(23, (L, MIX_WIDTH, D_MODEL), MIX_WIDTH ** -0.5 * DEEPNORM_BETA),
        'ln1_g': 1.0 + nrm(24, (L, D_MODEL), 0.02),
        'ln1_b': nrm(25, (L, D_MODEL), 0.02),
        'w_router': nrm(26, (L, D_MODEL, N_EXPERTS), D_MODEL ** -0.5),
        'router_bias': nrm(27, (L, N_EXPERTS), 0.01),
        'w_gate': nrm(28, (L, N_EXPERTS, D_MODEL, D_EXPERT), D_MODEL ** -0.5),
        'w_up': nrm(29, (L, N_EXPERTS, D_MODEL, D_EXPERT), D_MODEL ** -0.5),
        'w_down': nrm(30, (L, N_EXPERTS, D_EXPERT, D_MODEL), D_EXPERT ** -0.5 * DEEPNORM_BETA),
        'ws_gate': nrm(31, (L, D_MODEL, D_EXPERT), D_MODEL ** -0.5),
        'ws_up': nrm(32, (L, D_MODEL, D_EXPERT), D_MODEL ** -0.5),
        'ws_down': nrm(33, (L, D_EXPERT, D_MODEL), D_EXPERT ** -0.5 * DEEPNORM_BETA),
        'ln2_g': 1.0 + nrm(34, (L, D_MODEL), 0.02),
        'ln2_b': nrm(35, (L, D_MODEL), 0.02),
    }


def reference(x_prompt, x_sample, state_rwkv, state_shift, state_ret, cache_k_win, cache_v_win,
              w_in, mu_rkv, mu_wag, w0, w1, w2, a0, a1, a2, g1, g2, k_k, k_a, r_k, lnx_g, lnx_b,
              w_out, ln1_g, ln1_b, w_router, router_bias, w_gate, w_up, w_down,
              ws_gate, ws_up, ws_down, ln2_g, ln2_b):
    f32 = jnp.float32
    B_p, T_p, _ = x_prompt.shape
    pos_p = jnp.arange(T_p)
    pos_s = PAST_LEN + jnp.arange(x_sample.shape[1])

    def layer(l, x, pos, x_last, rwkv_S, ret_S, win_k, win_v):
        h, rwkv_S, x_last, ret_S, new_k, new_v = mixer_block(
            x, pos, x_last, rwkv_S, ret_S, win_k, win_v, w_in[l], mu_rkv[l], mu_wag[l],
            w0[l], w1[l], w2[l], a0[l], a1[l], a2[l], g1[l], g2[l], k_k[l], k_a[l], r_k[l],
            lnx_g[l], lnx_b[l], w_out[l])
        x = layer_norm(DEEPNORM_ALPHA * x + h, ln1_g[l], ln1_b[l])
        f = moe_block(x, w_router[l], router_bias[l], w_gate[l], w_up[l], w_down[l],
                      ws_gate[l], ws_up[l], ws_down[l])
        x = layer_norm(DEEPNORM_ALPHA * x + f, ln2_g[l], ln2_b[l])
        return x, (rwkv_S.astype(x.dtype), x_last, ret_S.astype(x.dtype), new_k, new_v)

    xp, xs = x_prompt, x_sample
    p_new, s_new = [], []
    for l in range(DEPTH):
        xp, st_p = layer(l, xp, pos_p, jnp.zeros((B_p, D_MODEL), xp.dtype),
                         jnp.zeros((B_p, H_RWKV, HEAD_DIM, HEAD_DIM), f32),
                         jnp.zeros((B_p, H_RET, HEAD_DIM, HEAD_DIM), f32), None, None)
        xs, st_s = layer(l, xs, pos_s, state_shift[l], state_rwkv[l], state_ret[l],
                         cache_k_win[l], cache_v_win[l])
        p_new.append(st_p)
        s_new.append(st_s)

    def stacked(states, i):
        return jnp.stack([st[i] for st in states])

    return (xp, xs,
            stacked(p_new, 0), stacked(p_new, 1), stacked(p_new, 2), stacked(p_new, 3), stacked(p_new, 4),
            stacked(s_new, 0), stacked(s_new, 1), stacked(s_new, 2), stacked(s_new, 3), stacked(s_new, 4))
```

```python
import functools
import math

import numpy as np
import jax
import jax.numpy as jnp
from jax import lax
from jax.experimental import pallas as pl
from jax.experimental.pallas import tpu as pltpu

f32 = jnp.float32
bf16 = jnp.bfloat16

D_MODEL = 1024
HEAD_DIM = 64
LANES = 128
H_RWKV = 5
H_RET = 5
H_DIL = 6
GROUP = 384
N_GROUPS_IN = 10
N_BLOCKS_IN = 3 * N_GROUPS_IN
C_RWKV = H_RWKV * HEAD_DIM
C_RET = H_RET * HEAD_DIM
C_DIL = H_DIL * HEAD_DIM
PAST_LEN = 2048
DEC_PAD = 8
RET_CHUNK = 128
RWKV_GN_EPS = 64e-5
RET_GN_EPS = 1e-5
DILATIONS = (1, 4, 16)
DIL_BLK = 128
ROPE_THETA = 10000.0
N_EXPERTS = 64
D_EXPERT = 128
TOP_K = 8
N_ROUTE_GROUPS = 8
TOPK_GROUPS = 4
ROUTE_SCALE = 2.5
LN_EPS = 1e-5
DEPTH = 2
DEEPNORM_ALPHA = (2 * DEPTH) ** 0.25
VMEM_LIMIT = 56 * 1024 * 1024
HIGHEST = lax.Precision.HIGHEST

_NT = (((1,), (1,)), ((), ()))
_TN = (((0,), (0,)), ((), ()))


def _cparams(sem):
    return pltpu.CompilerParams(dimension_semantics=sem, vmem_limit_bytes=VMEM_LIMIT)


def _lane_lo(shape):
    return (lax.broadcasted_iota(jnp.int32, shape, len(shape) - 1) % LANES) < HEAD_DIM


def _cat_blocks(ref):
    return jnp.concatenate([ref[0], ref[1], ref[2]], axis=1)


def _put_blocks(ref, val):
    for c in range(3):
        ref[c] = val[:, LANES * c:LANES * (c + 1)]


def _inproj_kernel(x_ref, w_ref, cr_ref, sr_ref, cd_ref, sd_ref, p_ref, kn_ref, vn_ref):
    acc = jnp.dot(x_ref[...].astype(bf16), w_ref[...], preferred_element_type=f32)
    tm = acc.shape[0]
    first = (lax.broadcasted_iota(jnp.int32, (tm, LANES), 1) % HEAD_DIM) < HEAD_DIM // 2

    def rot(xb, c, s):
        partner = jnp.where(first, pltpu.roll(xb, LANES - HEAD_DIM // 2, 1), pltpu.roll(xb, HEAD_DIM // 2, 1))
        return xb * c + partner * s

    for j in range(N_BLOCKS_IN):
        blk = acc[:, LANES * j:LANES * (j + 1)]
        g = j // 3
        if g in (3, 4):
            blk = rot(blk, cr_ref[...], sr_ref[...])
        if g in (7, 8):
            blk = rot(blk, cd_ref[...], sd_ref[...])
        if g in (4, 7):
            blk = blk * (HEAD_DIM ** -0.5)
        p_ref[j] = blk
        if g == 8:
            kn_ref[:, LANES * (j - 24):LANES * (j - 23)] = blk
        if g == 9:
            vn_ref[:, LANES * (j - 27):LANES * (j - 26)] = blk


def _inproj(x, w, cr, sr, cd, sd, tm):
    m = x.shape[0]
    tab = pl.BlockSpec((tm, LANES), lambda i: (i, 0))
    return pl.pallas_call(
        _inproj_kernel,
        grid=(m // tm,),
        in_specs=[pl.BlockSpec((tm, D_MODEL), lambda i: (i, 0)),
                  pl.BlockSpec((D_MODEL, N_BLOCKS_IN * LANES), lambda i: (0, 0)),
                  tab, tab, tab, tab],
        out_specs=[pl.BlockSpec((N_BLOCKS_IN, tm, LANES), lambda i: (0, i, 0)),
                   pl.BlockSpec((tm, GROUP), lambda i: (i, 0)),
                   pl.BlockSpec((tm, GROUP), lambda i: (i, 0))],
        out_shape=[jax.ShapeDtypeStruct((N_BLOCKS_IN, m, LANES), f32),
                   jax.ShapeDtypeStruct((m, GROUP), f32),
                   jax.ShapeDtypeStruct((m, GROUP), f32)],
        compiler_params=_cparams(("parallel",)),
        name="inproj",
    )(x, w, cr, sr, cd, sd)


def _softplus(z):
    return jnp.maximum(z, 0.0) + jnp.log(1.0 + jnp.exp(-jnp.abs(z)))


def _rwkv_prep_kernel(x_ref, xp_ref, r_ref, k_ref, v_ref, rp_ref, kp_ref, vp_ref,
                      mu_rkv_ref, mu_wag_ref, vec_ref, w1_ref, w2_ref, a1_ref, a2_ref, g1_ref, g2_ref, seg_ref,
                      ro_ref, wo_ref, ko_ref, vo_ref, kko_ref, bo_ref, go_ref, bon_ref):
    x = x_ref[...]
    xx = xp_ref[...] - x
    xw = (x + xx * mu_wag_ref[0:1, :]).astype(bf16)
    xa = (x + xx * mu_wag_ref[1:2, :]).astype(bf16)
    xg = (x + xx * mu_wag_ref[2:3, :]).astype(bf16)
    w0, a0, k_k, k_a, r_k = (vec_ref[i:i + 1, :] for i in range(5))

    lw = jnp.tanh(jnp.dot(xw, w1_ref[...], preferred_element_type=f32))
    wl = w0 + jnp.dot(lw.astype(bf16), w2_ref[...], preferred_element_type=f32)
    w_log = -_softplus(-wl) - 0.5
    decay = jnp.exp(-jnp.exp(w_log))
    la = jnp.dot(xa, a1_ref[...], preferred_element_type=f32)
    a = jax.nn.sigmoid(a0 + jnp.dot(la.astype(bf16), a2_ref[...], preferred_element_type=f32))
    lg = jax.nn.sigmoid(jnp.dot(xg, g1_ref[...], preferred_element_type=f32))
    g = jnp.dot(lg.astype(bf16), g2_ref[...], preferred_element_type=f32)

    def shifted(cur_ref, prev_ref, i):
        cur = _cat_blocks(cur_ref)
        return cur + mu_rkv_ref[i:i + 1, :] * (_cat_blocks(prev_ref) - cur)

    r = shifted(r_ref, rp_ref, 0)
    k = shifted(k_ref, kp_ref, 1)
    v = shifted(v_ref, vp_ref, 2)

    seg = seg_ref[...]
    kk = k * k_k
    ss = jnp.dot(kk * kk, seg, precision=HIGHEST, preferred_element_type=f32)
    kk = kk * lax.rsqrt(jnp.maximum(ss, 1e-24))
    k = k * (1.0 + (a - 1.0) * k_a)
    bonus = jnp.dot(r * k * r_k, seg, precision=HIGHEST, preferred_element_type=f32) * v

    _put_blocks(ro_ref, r)
    _put_blocks(wo_ref, decay)
    _put_blocks(ko_ref, k)
    _put_blocks(vo_ref, v)
    _put_blocks(kko_ref, kk)
    _put_blocks(bo_ref, kk * a)
    _put_blocks(go_ref, g)
    _put_blocks(bon_ref, bonus)


def _rwkv_prep(x, xp, p, pp, mu_rkv, mu_wag, vec, w1, w2, a1, a2, g1, g2, seg, tm):
    m = x.shape[0]
    row = pl.BlockSpec((tm, D_MODEL), lambda i: (i, 0))

    def grp(gi):
        return pl.BlockSpec((3, tm, LANES), lambda i: (gi, i, 0))

    def full(a):
        return pl.BlockSpec(a.shape, lambda i: (0,) * a.ndim)

    out = jax.ShapeDtypeStruct((3, m, LANES), f32)
    return pl.pallas_call(
        _rwkv_prep_kernel,
        grid=(m // tm,),
        in_specs=[row, row, grp(0), grp(1), grp(2), grp(0), grp(1), grp(2)]
        + [full(a) for a in (mu_rkv, mu_wag, vec, w1, w2, a1, a2, g1, g2, seg)],
        out_specs=[pl.BlockSpec((3, tm, LANES), lambda i: (0, i, 0))] * 8,
        out_shape=[out] * 8,
        compiler_params=_cparams(("parallel",)),
        name="rwkv_prep",
    )(x, xp, p, p, p, pp, pp, pp, mu_rkv, mu_wag, vec, w1, w2, a1, a2, g1, g2, seg)


def _rwkv_scan_kernel(r_ref, w_ref, k_ref, v_ref, kk_ref, b_ref, s0_ref, y_ref, so_ref, s_scr, *, n_steps):
    ti = pl.program_id(1)
    bb = s_scr.shape[0]

    @pl.when(ti == 0)
    def _():
        s_scr[...] = s0_ref[...]

    y_ref[...] = jnp.zeros(y_ref.shape, f32)
    eye = (lax.broadcasted_iota(jnp.int32, (HEAD_DIM, HEAD_DIM), 0)
           == lax.broadcasted_iota(jnp.int32, (HEAD_DIM, HEAD_DIM), 1))

    def step(t, carry):
        for b in range(bb):
            for c in range(3):
                rows = [ref[c, b, pl.ds(t, 1), :] for ref in (r_ref, w_ref, k_ref, v_ref, kk_ref, b_ref)]
                ys = []
                for half in range(2):
                    h = 2 * c + half
                    if h >= H_RWKV:
                        ys.append(jnp.zeros((1, HEAD_DIM), f32))
                        continue
                    r_t, w_t, k_t, v_t, kk_t, b_t = (x[:, HEAD_DIM * half:HEAD_DIM * (half + 1)] for x in rows)
                    s = s_scr[b, h]
                    sa = jnp.sum(s * kk_t, axis=-1, keepdims=True)
                    v_col = jnp.sum(jnp.where(eye, v_t, 0.0), axis=-1, keepdims=True)
                    s = s * w_t - sa * b_t + v_col * k_t
                    s_scr[b, h] = s
                    y_col = jnp.sum(s * r_t, axis=-1, keepdims=True)
                    ys.append(jnp.sum(jnp.where(eye, y_col, 0.0), axis=0, keepdims=True))
                y_ref[c, b, pl.ds(t, 1), :] = jnp.concatenate(ys, axis=1)
        return carry

    lax.fori_loop(0, n_steps, step, 0)

    @pl.when(ti == pl.num_programs(1) - 1)
    def _():
        so_ref[...] = s_scr[...]


def _rwkv_scan(ops, s0, bb, tc, n_steps):
    _, b, t, _ = ops[0].shape
    blk = pl.BlockSpec((3, bb, tc, LANES), lambda bi, ti: (0, bi, ti, 0))
    st = pl.BlockSpec((bb, H_RWKV, HEAD_DIM, HEAD_DIM), lambda bi, ti: (bi, 0, 0, 0))
    return pl.pallas_call(
        functools.partial(_rwkv_scan_kernel, n_steps=n_steps),
        grid=(b // bb, t // tc),
        in_specs=[blk] * 6 + [st],
        out_specs=[blk, st],
        out_shape=[jax.ShapeDtypeStruct((3, b, t, LANES), f32),
                   jax.ShapeDtypeStruct((b, H_RWKV, HEAD_DIM, HEAD_DIM), f32)],
        scratch_shapes=[pltpu.VMEM((bb, H_RWKV, HEAD_DIM, HEAD_DIM), f32)],
        compiler_params=_cparams(("parallel", "arbitrary")),
        name="rwkv_scan",
    )(*ops, s0)


def _log_gamma(h):
    return float(np.log(np.float32(1.0) - np.float32(2.0) ** np.float32(-5.0 - min(h, H_RET - 1))))


def _retention_kernel(q_ref, k_ref, v_ref, g_ref, s0_ref, seg_ref, y_ref, so_ref, s_scr, *, n_valid):
    ci = pl.program_id(1)
    rows = q_ref.shape[1]

    @pl.when(ci == 0)
    def _():
        s_scr[...] = s0_ref[0]

    lo = _lane_lo((rows, LANES))
    ii = lax.broadcasted_iota(jnp.int32, (rows, rows), 0)
    jj = lax.broadcasted_iota(jnp.int32, (rows, rows), 1)
    rel = (ii - jj).astype(f32)
    causal = ii >= jj
    pos = lax.broadcasted_iota(jnp.int32, (rows, LANES), 0)
    posf = pos.astype(f32)
    row_ok = pos < n_valid
    sq_r = lax.broadcasted_iota(jnp.int32, (LANES, LANES), 0) < HEAD_DIM
    sq_c = lax.broadcasted_iota(jnp.int32, (LANES, LANES), 1) < HEAD_DIM
    same_head = sq_r == sq_c
    seg = seg_ref[...]

    for c in range(3):
        lg0, lg1 = _log_gamma(2 * c), _log_gamma(2 * c + 1)
        lg_lane = jnp.where(lo, lg0, lg1)
        q = q_ref[c]
        q16 = q.astype(bf16)
        k = jnp.where(row_ok, k_ref[c], 0.0)
        k16 = k.astype(bf16)
        v16 = v_ref[c].astype(bf16)
        o_halves = []
        for half, lg in ((0, lg0), (1, lg1)):
            qm = jnp.where(lo if half == 0 else jnp.logical_not(lo), q, 0.0).astype(bf16)
            s = lax.dot_general(qm, k16, _NT, preferred_element_type=f32)
            dmat = jnp.where(causal, jnp.exp(lg * jnp.maximum(rel, 0.0)), 0.0)
            o_halves.append(jnp.dot((s * dmat).astype(bf16), v16, preferred_element_type=f32))
        o = jnp.where(lo, o_halves[0], o_halves[1])
        state = s_scr[c]
        cross = jnp.dot(q16, state.astype(bf16), preferred_element_type=f32)
        o = o + cross * jnp.exp(lg_lane * (posf + 1.0))
        k_in = (k * jnp.exp(lg_lane * (float(n_valid) - 1.0 - posf))).astype(bf16)
        upd = lax.dot_general(k_in, v16, _TN, preferred_element_type=f32)
        lg_col = jnp.where(sq_c, lg0, lg1)
        s_scr[c] = state * jnp.exp(lg_col * float(n_valid)) + jnp.where(same_head, upd, 0.0)

        mean = jnp.dot(o, seg, precision=HIGHEST, preferred_element_type=f32)
        d = o - mean
        var = jnp.dot(d * d, seg, precision=HIGHEST, preferred_element_type=f32)
        yn = d * lax.rsqrt(var + RET_GN_EPS)
        g = g_ref[c]
        y_ref[c] = g * jax.nn.sigmoid(g) * yn

    @pl.when(ci == pl.num_programs(1) - 1)
    def _():
        so_ref[0] = s_scr[...]


def _retention(p, s0_pairs, seg, n_seq, rows, n_chunks, n_valid, row0):
    def grp(gi):
        return pl.BlockSpec((3, rows, LANES), lambda b, c: (gi, row0 // rows + b * n_chunks + c, 0))

    st = pl.BlockSpec((1, 3, LANES, LANES), lambda b, c: (b, 0, 0, 0))
    return pl.pallas_call(
        functools.partial(_retention_kernel, n_valid=n_valid),
        grid=(n_seq, n_chunks),
        in_specs=[grp(3), grp(4), grp(5), grp(6), st, pl.BlockSpec(seg.shape, lambda b, c: (0, 0))],
        out_specs=[pl.BlockSpec((3, rows, LANES), lambda b, c: (0, b * n_chunks + c, 0)), st],
        out_shape=[jax.ShapeDtypeStruct((3, n_seq * n_chunks * rows, LANES), f32),
                   jax.ShapeDtypeStruct((n_seq, 3, LANES, LANES), f32)],
        scratch_shapes=[pltpu.VMEM((3, LANES, LANES), f32)],
        compiler_params=_cparams(("parallel", "arbitrary")),
        name="retention",
    )(p, p, p, p, s0_pairs, seg)


def _dil_prompt_kernel(q_ref, k_ref, v_ref, y_ref, o_scr, l_scr):
    i = pl.program_id(1)
    n_steps = pl.num_programs(1)
    blk = DIL_BLK
    lo = _lane_lo((blk, LANES))
    ii = lax.broadcasted_iota(jnp.int32, (blk, 2 * blk), 0)
    jj = lax.broadcasted_iota(jnp.int32, (blk, 2 * blk), 1)
    band = jnp.logical_and(jj >= ii, jj <= ii + blk)

    for p, dil in enumerate(DILATIONS):
        nb = q_ref.shape[1] // (dil * blk)
        r = i // nb
        cb = i % nb
        start_q = r + dil * blk * cb
        start_p = r + dil * blk * jnp.maximum(cb - 1, 0)
        valid = jnp.logical_and(band, jnp.logical_or(cb > 0, jj >= blk))

        def rows(ref, c, start):
            if dil == 1:
                return ref[c, pl.ds(start, blk), :]
            return ref[c, pl.ds(start, blk, stride=dil), :]

        for c in range(3):
            q = rows(q_ref, c, start_q)
            kwin = jnp.concatenate([rows(k_ref, c, start_p), rows(k_ref, c, start_q)], axis=0).astype(bf16)
            vwin = jnp.concatenate([rows(v_ref, c, start_p), rows(v_ref, c, start_q)], axis=0).astype(bf16)
            outs, lses = [], []
            for half in range(2):
                qm = jnp.where(lo if half == 0 else jnp.logical_not(lo), q, 0.0).astype(bf16)
                s = lax.dot_general(qm, kwin, _NT, preferred_element_type=f32)
                s = jnp.where(valid, s, -jnp.inf)
                mx = jnp.max(s, axis=-1, keepdims=True)
                lse = mx + jnp.log(jnp.sum(jnp.exp(s - mx), axis=-1, keepdims=True))
                prob = jnp.exp(s - lse).astype(bf16)
                outs.append(jnp.dot(prob, vwin, preferred_element_type=f32))
                lses.append(lse)
            o_pair = jnp.where(lo, outs[0], outs[1])
            l_pair = jnp.where(lo, lses[0], lses[1])
            if dil == 1:
                o_scr[p, c, pl.ds(start_q, blk), :] = o_pair
                l_scr[p, c, pl.ds(start_q, blk), :] = l_pair
            else:
                o_scr[p, c, pl.ds(start_q, blk, stride=dil), :] = o_pair
                l_scr[p, c, pl.ds(start_q, blk, stride=dil), :] = l_pair

    @pl.when(i == n_steps - 1)
    def _():
        def merge(j, carry):
            sl = pl.ds(pl.multiple_of(j * blk, blk), blk)
            for c in range(3):
                ls = [l_scr[p, c, sl, :] for p in range(3)]
                mx = jnp.maximum(jnp.maximum(ls[0], ls[1]), ls[2])
                es = [jnp.exp(l - mx) for l in ls]
                tot = es[0] + es[1] + es[2]
                acc = (es[0] / tot) * o_scr[0, c, sl, :]
                acc = acc + (es[1] / tot) * o_scr[1, c, sl, :]
                acc = acc + (es[2] / tot) * o_scr[2, c, sl, :]
                y_ref[c, sl, :] = acc
            return carry

        lax.fori_loop(0, q_ref.shape[1] // blk, merge, 0)


def _dil_prompt(p, n_seq, t):
    def grp(gi):
        return pl.BlockSpec((3, t, LANES), lambda b, i: (gi, b, 0))

    n_steps = t // DIL_BLK
    return pl.pallas_call(
        _dil_prompt_kernel,
        grid=(n_seq, n_steps),
        in_specs=[grp(7), grp(8), grp(9)],
        out_specs=pl.BlockSpec((3, t, LANES), lambda b, i: (0, b, 0)),
        out_shape=jax.ShapeDtypeStruct((3, n_seq * t, LANES), f32),
        scratch_shapes=[pltpu.VMEM((3, 3, t, LANES), f32), pltpu.VMEM((3, 3, t, LANES), f32)],
        compiler_params=_cparams(("parallel", "arbitrary")),
        name="dil_prompt",
    )(p, p, p)


N_NEW = 4
WIN_KEYS = 2 * DIL_BLK


def _dil_sample_kernel(q_ref, kn_ref, vn_ref, kc_ref, vc_ref, y_ref, ko_ref, vo_ref, kall, vall, *, n_buf):
    n_rows = 3 * n_buf
    tail0 = (n_rows + 3 * N_NEW) // 8 * 8

    for new_ref, cache_ref, out_ref, scr in ((kn_ref, kc_ref, ko_ref, kall), (vn_ref, vc_ref, vo_ref, vall)):
        @pl.when(pl.program_id(0) == 0)
        def _(scr=scr):
            scr[pl.ds(tail0, scr.shape[0] - tail0), :] = jnp.zeros((scr.shape[0] - tail0, LANES), f32)

        scr[pl.ds(0, n_rows), :] = cache_ref[0]
        for t in range(N_NEW):
            for c in range(3):
                scr[pl.ds(n_rows + 3 * t + c, 1), :] = new_ref[c, pl.ds(t, 1), :]
        out_ref[0] = scr[pl.ds(3 * N_NEW, n_rows), :]

    lo8 = _lane_lo((8, LANES))
    sub8 = lax.broadcasted_iota(jnp.int32, (8, LANES), 0)
    key_ok = lax.broadcasted_iota(jnp.int32, (8, WIN_KEYS), 1) <= DIL_BLK
    y_ref[...] = jnp.zeros(y_ref.shape, f32)

    for t in range(N_NEW):
        for c in range(3):
            q_row = q_ref[c, pl.ds(t, 1), :]
            qm = jnp.where(jnp.logical_or(jnp.logical_and(sub8 == 0, lo8),
                                          jnp.logical_and(sub8 == 1, jnp.logical_not(lo8))), q_row, 0.0).astype(bf16)
            outs, lses = [], []
            for dil in DILATIONS:
                start = 3 * (n_buf + t - dil * DIL_BLK) + c
                kwin = kall[pl.ds(start, WIN_KEYS, stride=3 * dil), :].astype(bf16)
                vwin = vall[pl.ds(start, WIN_KEYS, stride=3 * dil), :].astype(bf16)
                s = lax.dot_general(qm, kwin, _NT, preferred_element_type=f32)
                s = jnp.where(key_ok, s, -jnp.inf)
                mx = jnp.max(s, axis=-1, keepdims=True)
                lse = mx + jnp.log(jnp.sum(jnp.exp(s - mx), axis=-1, keepdims=True))
                prob = jnp.exp(s - lse).astype(bf16)
                outs.append(jnp.dot(prob, vwin, preferred_element_type=f32))
                lses.append(lse)
            mx = jnp.maximum(jnp.maximum(lses[0], lses[1]), lses[2])
            es = [jnp.exp(l - mx) for l in lses]
            tot = es[0] + es[1] + es[2]
            acc = (es[0] / tot) * outs[0] + (es[1] / tot) * outs[1] + (es[2] / tot) * outs[2]
            y_ref[c, pl.ds(t, 1), :] = jnp.where(lo8[0:1, :], acc[0:1, :], acc[1:2, :])


def _dil_sample(p, cache_k, cache_v, n_seq, row0, n_buf):
    n_rows = 3 * n_buf
    scr_rows = -(-(3 * (n_buf + N_NEW + DILATIONS[-1] * (WIN_KEYS - DIL_BLK)) + 8) // 8) * 8

    def grp(gi):
        return pl.BlockSpec((3, DEC_PAD, LANES), lambda b: (gi, row0 // DEC_PAD + b, 0))

    buf = pl.BlockSpec((1, n_rows, LANES), lambda b: (b, 0, 0))
    return pl.pallas_call(
        functools.partial(_dil_sample_kernel, n_buf=n_buf),
        grid=(n_seq,),
        in_specs=[grp(7), grp(8), grp(9), buf, buf],
        out_specs=[pl.BlockSpec((3, DEC_PAD, LANES), lambda b: (0, b, 0)), buf, buf],
        out_shape=[jax.ShapeDtypeStruct((3, n_seq * DEC_PAD, LANES), f32),
                   jax.ShapeDtypeStruct((n_seq, n_rows, LANES), f32),
                   jax.ShapeDtypeStruct((n_seq, n_rows, LANES), f32)],
        scratch_shapes=[pltpu.VMEM((scr_rows, LANES), f32), pltpu.VMEM((scr_rows, LANES), f32)],
        compiler_params=_cparams(("arbitrary",)),
        name="dil_sample",
    )(p, p, p, cache_k, cache_v)


def _layer_norm(z, gamma, beta):
    mu = jnp.mean(z, axis=-1, keepdims=True)
    d = z - mu
    var = jnp.mean(d * d, axis=-1, keepdims=True)
    return d * lax.rsqrt(var + LN_EPS) * gamma + beta


def _merge_kernel(ya_ref, bon_ref, g_ref, yb_ref, yc_ref, x_ref, lnx_ref, seg_ref, wo_ref, ln_ref, o_ref):
    yr = _cat_blocks(ya_ref)
    seg = seg_ref[...]
    mean = jnp.dot(yr, seg, precision=HIGHEST, preferred_element_type=f32)
    d = yr - mean
    var = jnp.dot(d * d, seg, precision=HIGHEST, preferred_element_type=f32)
    ya = d * lax.rsqrt(var + RWKV_GN_EPS) * lnx_ref[0:1, :] + lnx_ref[1:2, :]
    ya = (ya + _cat_blocks(bon_ref)) * _cat_blocks(g_ref)
    ycat = jnp.concatenate([ya, _cat_blocks(yb_ref), _cat_blocks(yc_ref)], axis=1).astype(bf16)
    h = jnp.dot(ycat, wo_ref[...], preferred_element_type=f32)
    o_ref[...] = _layer_norm(DEEPNORM_ALPHA * x_ref[...] + h, ln_ref[0:1, :], ln_ref[1:2, :])


def _merge(ya, bonus, g, yb, yc, x, lnx, seg, wo, ln, tm):
    m = x.shape[0]
    blk = pl.BlockSpec((3, tm, LANES), lambda i: (0, i, 0))

    def full(a):
        return pl.BlockSpec(a.shape, lambda i: (0,) * a.ndim)

    return pl.pallas_call(
        _merge_kernel,
        grid=(m // tm,),
        in_specs=[blk] * 5 + [pl.BlockSpec((tm, D_MODEL), lambda i: (i, 0)), full(lnx), full(seg), full(wo), full(ln)],
        out_specs=pl.BlockSpec((tm, D_MODEL), lambda i: (i, 0)),
        out_shape=jax.ShapeDtypeStruct((m, D_MODEL), f32),
        compiler_params=_cparams(("parallel",)),
        name="mixer_merge",
    )(ya, bonus, g, yb, yc, x, lnx, seg, wo, ln)


EXPERT_BLOCK = 8


def _first_max(x, idx, axes, sentinel):
    mx = x
    for ax in axes:
        mx = jnp.max(mx, axis=ax, keepdims=True)
    am = jnp.where(x == mx, idx, sentinel)
    for ax in axes:
        am = jnp.min(am, axis=ax, keepdims=True)
    return mx, am


def _route(logits_t, bias_col):
    n_tok = logits_t.shape[1]
    per = N_EXPERTS // N_ROUTE_GROUPS
    scores = jax.nn.sigmoid(logits_t)
    shp = (N_ROUTE_GROUPS, per, n_tok)
    s3 = scores.reshape(shp)
    b3 = (scores + bias_col).reshape(shp)
    sub = lax.broadcasted_iota(jnp.int32, shp, 1)
    grp = lax.broadcasted_iota(jnp.int32, shp, 0)
    eid = grp * per + sub
    neg = -jnp.inf

    m1, i1 = _first_max(b3, sub, (1,), per)
    m2, _ = _first_max(jnp.where(sub == i1, neg, b3), sub, (1,), per)
    gscore = m1 + m2
    gid = lax.broadcasted_iota(jnp.int32, gscore.shape, 0)
    gsel = jnp.zeros(gscore.shape, jnp.bool_)
    for _ in range(TOPK_GROUPS):
        _, gi = _first_max(gscore, gid, (0,), N_ROUTE_GROUPS)
        hit = gid == gi
        gsel = jnp.logical_or(gsel, hit)
        gscore = jnp.where(hit, neg, gscore)
    masked = jnp.where(gsel, b3, neg)
    esel = jnp.zeros(shp, jnp.bool_)
    for _ in range(TOP_K):
        _, ei = _first_max(masked, eid, (1, 0), N_EXPERTS)
        hit = eid == ei
        esel = jnp.logical_or(esel, hit)
        masked = jnp.where(hit, neg, masked)
    top_s = jnp.where(esel, s3, 0.0)
    denom = jnp.sum(jnp.sum(top_s, axis=1, keepdims=True), axis=0, keepdims=True)
    return (top_s / denom * ROUTE_SCALE).reshape(N_EXPERTS, n_tok)


def _moe_kernel(x_ref, wr_ref, rb_ref, wg_ref, wu_ref, wd_ref, ex_ref, wsg_ref, wsu_ref, wsd_ref, ln_ref,
                o_ref, x16_scr, chi_scr, clo_scr, acc_scr):
    j = pl.program_id(1)

    @pl.when(j == 0)
    def _():
        x16 = x_ref[...].astype(bf16)
        x16_scr[...] = x16
        logits_t = lax.dot_general(wr_ref[...], x16, _NT, preferred_element_type=f32)
        comb_t = _route(logits_t, rb_ref[...])
        comb_t = jnp.concatenate([comb_t, jnp.zeros_like(comb_t)], axis=0)
        comb = comb_t.T
        hi = comb.astype(bf16)
        chi_scr[...] = hi
        clo_scr[...] = (comb - hi.astype(f32)).astype(bf16)
        sg = jnp.dot(x16, wsg_ref[...], preferred_element_type=f32)
        su = jnp.dot(x16, wsu_ref[...], preferred_element_type=f32)
        hs = (sg * jax.nn.sigmoid(sg) * su).astype(bf16)
        acc_scr[...] = jnp.dot(hs, wsd_ref[...], preferred_element_type=f32)

    x16 = x16_scr[...]
    gate = jnp.dot(x16, wg_ref[...], preferred_element_type=f32)
    up = jnp.dot(x16, wu_ref[...], preferred_element_type=f32)
    ex = ex_ref[...]
    cexp = (jnp.dot(chi_scr[...], ex, preferred_element_type=f32)
            + jnp.dot(clo_scr[...], ex, preferred_element_type=f32))
    h = (gate * jax.nn.sigmoid(gate) * up * cexp).astype(bf16)
    acc_scr[...] += jnp.dot(h, wd_ref[...], preferred_element_type=f32)

    @pl.when(j == pl.num_programs(1) - 1)
    def _():
        o_ref[...] = _layer_norm(DEEPNORM_ALPHA * x_ref[...] + acc_scr[...], ln_ref[0:1, :], ln_ref[1:2, :])


def _moe(x, wr_t, rb_col, wg, wu, wd, expand, wsg, wsu, wsd, ln, tm):
    m = x.shape[0]
    nb = EXPERT_BLOCK * D_EXPERT

    def full(a):
        return pl.BlockSpec(a.shape, lambda i, j: (0,) * a.ndim)

    return pl.pallas_call(
        _moe_kernel,
        grid=(m // tm, N_EXPERTS // EXPERT_BLOCK),
        in_specs=[pl.BlockSpec((tm, D_MODEL), lambda i, j: (i, 0)), full(wr_t), full(rb_col),
                  pl.BlockSpec((D_MODEL, nb), lambda i, j: (0, j)),
                  pl.BlockSpec((D_MODEL, nb), lambda i, j: (0, j)),
                  pl.BlockSpec((nb, D_MODEL), lambda i, j: (j, 0)),
                  pl.BlockSpec((LANES, nb), lambda i, j: (0, j)),
                  full(wsg), full(wsu), full(wsd), full(ln)],
        out_specs=pl.BlockSpec((tm, D_MODEL), lambda i, j: (i, 0)),
        out_shape=jax.ShapeDtypeStruct((m, D_MODEL), f32),
        scratch_shapes=[pltpu.VMEM((tm, D_MODEL), bf16), pltpu.VMEM((tm, LANES), bf16),
                        pltpu.VMEM((tm, LANES), bf16), pltpu.VMEM((tm, D_MODEL), f32)],
        compiler_params=_cparams(("parallel", "arbitrary")),
        name="moe",
    )(x, wr_t, rb_col, wg, wu, wd, expand, wsg, wsu, wsd, ln)


def _pad_heads(a, n_heads):
    return jnp.pad(a, [(0, 0)] * (a.ndim - 1) + [(0, GROUP - n_heads * HEAD_DIM)])


def _pad_w_in(w):
    widths = [C_RWKV] * 3 + [C_RET] * 4 + [C_DIL] * 3
    parts, off = [], 0
    for wd in widths:
        parts.append(jnp.pad(w[:, off:off + wd], ((0, 0), (0, GROUP - wd))))
        off += wd
    return jnp.concatenate(parts, axis=1).astype(bf16)


def _rope_tables(pos, inv_freq):
    ang = pos.astype(f32)[:, None] * inv_freq[None, :]
    cos, sin = jnp.cos(ang), jnp.sin(ang)
    return jnp.tile(jnp.concatenate([cos, cos], axis=1), (1, 2)), jnp.tile(jnp.concatenate([-sin, sin], axis=1), (1, 2))


def _segment_matrix(width, scale):
    idx = np.arange(width) // HEAD_DIM
    return jnp.asarray((idx[:, None] == idx[None, :]).astype(np.float32) * scale)


def _pairs_from_heads(s):
    b = s.shape[0]
    s6 = jnp.pad(s, ((0, 0), (0, 1), (0, 0), (0, 0))).reshape(b, 3, 2, HEAD_DIM, HEAD_DIM)
    z = jnp.zeros((b, 3, HEAD_DIM, HEAD_DIM), s.dtype)
    top = jnp.concatenate([s6[:, :, 0], z], axis=-1)
    bot = jnp.concatenate([z, s6[:, :, 1]], axis=-1)
    return jnp.concatenate([top, bot], axis=-2)


def _heads_from_pairs(sp):
    b = sp.shape[0]
    h0 = sp[:, :, :HEAD_DIM, :HEAD_DIM]
    h1 = sp[:, :, HEAD_DIM:, HEAD_DIM:]
    return jnp.stack([h0, h1], axis=2).reshape(b, 6, HEAD_DIM, HEAD_DIM)[:, :H_RET]


def kernel(x_prompt, x_sample, state_rwkv, state_shift, state_ret, cache_k_win, cache_v_win, w_in, mu_rkv, mu_wag, w0, w1, w2, a0, a1, a2, g1, g2, k_k, k_a, r_k, lnx_g, lnx_b, w_out, ln1_g, ln1_b, w_router, router_bias, w_gate, w_up, w_down, ws_gate, ws_up, ws_down, ln2_g, ln2_b):
    bp, tp, _ = x_prompt.shape
    bs, ts, _ = x_sample.shape
    n_buf = cache_k_win.shape[2]
    depth = w_in.shape[0]
    assert ts == N_NEW and n_buf == DILATIONS[-1] * DIL_BLK and tp % (DILATIONS[-1] * DIL_BLK) == 0
    mp = bp * tp
    ms = bs * DEC_PAD
    m = mp + ms
    tm = 256
    tm_moe = 512
    assert m % tm_moe == 0 and bs % 8 == 0

    pos = jnp.concatenate([jnp.tile(jnp.arange(tp), bp), jnp.tile(PAST_LEN + jnp.arange(DEC_PAD), bs)])
    inv_ret = 1.0 / (10000.0 ** jnp.linspace(0.0, 1.0, HEAD_DIM // 2, dtype=f32))
    inv_dil = ROPE_THETA ** (-jnp.arange(0, HEAD_DIM, 2, dtype=f32) / HEAD_DIM)
    cr, sr = _rope_tables(pos, inv_ret)
    cd, sd = _rope_tables(pos, inv_dil)
    zero_tab = jnp.zeros((bs, LANES), f32)

    seg_sum = _segment_matrix(GROUP, 1.0)
    seg_mean = _segment_matrix(GROUP, 1.0 / HEAD_DIM)
    seg_mean_pair = _segment_matrix(LANES, 1.0 / HEAD_DIM)
    expand = jnp.asarray((np.arange(LANES)[:, None] == (np.arange(N_EXPERTS * D_EXPERT) // D_EXPERT)[None, :])
                         .astype(np.float32)).astype(bf16)

    xs_pad = jnp.pad(x_sample, ((0, 0), (0, DEC_PAD - ts), (0, 0)))
    x = jnp.concatenate([x_prompt.reshape(mp, D_MODEL), xs_pad.reshape(ms, D_MODEL)], axis=0)

    outs_p = [[] for _ in range(5)]
    outs_s = [[] for _ in range(5)]
    for l in range(depth):
        xp3 = x[:mp].reshape(bp, tp, D_MODEL)
        xs3 = x[mp:].reshape(bs, DEC_PAD, D_MODEL)
        x_last_s = state_shift[l]
        outs_p[1].append(xp3[:, -1])
        outs_s[1].append(xs3[:, ts - 1])
        x_prev = jnp.concatenate([
            jnp.concatenate([jnp.zeros((bp, 1, D_MODEL), f32), xp3[:, :-1]], axis=1).reshape(mp, D_MODEL),
            jnp.concatenate([x_last_s[:, None], xs3[:, :-1]], axis=1).reshape(ms, D_MODEL)], axis=0)

        w_in_p = _pad_w_in(w_in[l])
        p, k_new, v_new = _inproj(x, w_in_p, cr, sr, cd, sd, tm)
        p_last, _, _ = _inproj(x_last_s, w_in_p, zero_tab, zero_tab, zero_tab, zero_tab, min(bs, 128))

        pr = p[:9]
        pr_p = pr[:, :mp].reshape(9, bp, tp, LANES)
        pr_s = pr[:, mp:].reshape(9, bs, DEC_PAD, LANES)
        pp = jnp.concatenate([
            jnp.concatenate([jnp.zeros((9, bp, 1, LANES), f32), pr_p[:, :, :-1]], axis=2).reshape(9, mp, LANES),
            jnp.concatenate([p_last[:9][:, :, None], pr_s[:, :, :-1]], axis=2).reshape(9, ms, LANES)], axis=1)

        vec = jnp.stack([_pad_heads(v, H_RWKV) for v in (w0[l], a0[l], k_k[l], k_a[l], r_k[l].reshape(-1))])
        prep = _rwkv_prep(x, x_prev, p, pp, _pad_heads(mu_rkv[l], H_RWKV), mu_wag[l], vec,
                          w1[l].astype(bf16), _pad_heads(w2[l], H_RWKV).astype(bf16),
                          a1[l].astype(bf16), _pad_heads(a2[l], H_RWKV).astype(bf16),
                          g1[l].astype(bf16), _pad_heads(g2[l], H_RWKV).astype(bf16), seg_sum, tm)
        scan_ops, gate_a, bonus = prep[:6], prep[6], prep[7]

        ops_p = [a[:, :mp].reshape(3, bp, tp, LANES) for a in scan_ops]
        ops_s = [a[:, mp:].reshape(3, bs, DEC_PAD, LANES) for a in scan_ops]
        ya_p, s_rwkv_p = _rwkv_scan(ops_p, jnp.zeros((bp, H_RWKV, HEAD_DIM, HEAD_DIM), f32), bp, 64, 64)
        ya_s, s_rwkv_s = _rwkv_scan(ops_s, state_rwkv[l], 8, DEC_PAD, ts)
        ya = jnp.concatenate([ya_p.reshape(3, mp, LANES), ya_s.reshape(3, ms, LANES)], axis=1)
        outs_p[0].append(s_rwkv_p)
        outs_s[0].append(s_rwkv_s)

        yb_p, s_ret_p = _retention(p, jnp.zeros((bp, 3, LANES, LANES), f32), seg_mean_pair,
                                   bp, RET_CHUNK, tp // RET_CHUNK, RET_CHUNK, 0)
        yb_s, s_ret_s = _retention(p, _pairs_from_heads(state_ret[l]), seg_mean_pair, bs, DEC_PAD, 1, ts, mp)
        yb = jnp.concatenate([yb_p, yb_s], axis=1)
        outs_p[2].append(_heads_from_pairs(s_ret_p))
        outs_s[2].append(_heads_from_pairs(s_ret_s))

        yc_p = _dil_prompt(p, bp, tp)
        yc_s, k_win, v_win = _dil_sample(p, cache_k_win[l].reshape(bs, 3 * n_buf, LANES),
                                         cache_v_win[l].reshape(bs, 3 * n_buf, LANES), bs, mp, n_buf)
        yc = jnp.concatenate([yc_p, yc_s], axis=1)
        keep = min(n_buf, tp)
        outs_p[3].append(k_new[:mp].reshape(bp, tp, H_DIL, HEAD_DIM)[:, tp - keep:])
        outs_p[4].append(v_new[:mp].reshape(bp, tp, H_DIL, HEAD_DIM)[:, tp - keep:])
        outs_s[3].append(k_win.reshape(bs, n_buf, H_DIL, HEAD_DIM))
        outs_s[4].append(v_win.reshape(bs, n_buf, H_DIL, HEAD_DIM))

        w_out_p = jnp.concatenate([
            jnp.pad(w_out[l][:C_RWKV], ((0, GROUP - C_RWKV), (0, 0))),
            jnp.pad(w_out[l][C_RWKV:C_RWKV + C_RET], ((0, GROUP - C_RET), (0, 0))),
            w_out[l][C_RWKV + C_RET:]], axis=0).astype(bf16)
        lnx = jnp.stack([_pad_heads(lnx_g[l], H_RWKV), _pad_heads(lnx_b[l], H_RWKV)])
        x = _merge(ya, bonus, gate_a, yb, yc, x, lnx, seg_mean, w_out_p,
                   jnp.stack([ln1_g[l], ln1_b[l]]), tm)

        wg = jnp.transpose(w_gate[l], (1, 0, 2)).reshape(D_MODEL, N_EXPERTS * D_EXPERT).astype(bf16)
        wu = jnp.transpose(w_up[l], (1, 0, 2)).reshape(D_MODEL, N_EXPERTS * D_EXPERT).astype(bf16)
        wd = w_down[l].reshape(N_EXPERTS * D_EXPERT, D_MODEL).astype(bf16)
        x = _moe(x, w_router[l].T.astype(bf16), router_bias[l].reshape(N_EXPERTS, 1), wg, wu, wd, expand,
                 ws_gate[l].astype(bf16), ws_up[l].astype(bf16), ws_down[l].astype(bf16),
                 jnp.stack([ln2_g[l], ln2_b[l]]), tm_moe)

    y_prompt = x[:mp].reshape(bp, tp, D_MODEL)
    y_sample = x[mp:].reshape(bs, DEC_PAD, D_MODEL)[:, :ts]
    return (y_prompt, y_sample) + tuple(jnp.stack(o) for o in outs_p) + tuple(jnp.stack(o) for o in outs_s)
```

```python
import functools
import math

import numpy as np
import jax
import jax.numpy as jnp
from jax import lax
from jax.experimental import pallas as pl
from jax.experimental.pallas import tpu as pltpu

f32 = jnp.float32
bf16 = jnp.bfloat16

D_MODEL = 1024
HEAD_DIM = 64
LANES = 128
H_RWKV = 5
H_RET = 5
H_DIL = 6
GROUP = 384
N_GROUPS_IN = 10
N_BLOCKS_IN = 3 * N_GROUPS_IN
C_RWKV = H_RWKV * HEAD_DIM
C_RET = H_RET * HEAD_DIM
C_DIL = H_DIL * HEAD_DIM
PAST_LEN = 2048
DEC_PAD = 8
RET_CHUNK = 128
RWKV_GN_EPS = 64e-5
RET_GN_EPS = 1e-5
DILATIONS = (1, 4, 16)
DIL_BLK = 128
ROPE_THETA = 10000.0
N_EXPERTS = 64
D_EXPERT = 128
TOP_K = 8
N_ROUTE_GROUPS = 8
TOPK_GROUPS = 4
ROUTE_SCALE = 2.5
LN_EPS = 1e-5
DEPTH = 2
DEEPNORM_ALPHA = (2 * DEPTH) ** 0.25
VMEM_LIMIT = 56 * 1024 * 1024
HIGHEST = lax.Precision.HIGHEST

_NT = (((1,), (1,)), ((), ()))
_TN = (((0,), (0,)), ((), ()))


def _cparams(sem):
    return pltpu.CompilerParams(dimension_semantics=sem, vmem_limit_bytes=VMEM_LIMIT)


def _lane_lo(shape):
    return (lax.broadcasted_iota(jnp.int32, shape, len(shape) - 1) % LANES) < HEAD_DIM


def _cat_blocks(ref):
    return jnp.concatenate([ref[0], ref[1], ref[2]], axis=1)


def _put_blocks(ref, val):
    for c in range(3):
        ref[c] = val[:, LANES * c:LANES * (c + 1)]


def _inproj_kernel(x_ref, w_ref, cr_ref, sr_ref, cd_ref, sd_ref, p_ref, kn_ref, vn_ref):
    acc = jnp.dot(x_ref[...].astype(bf16), w_ref[...], preferred_element_type=f32)
    tm = acc.shape[0]
    first = (lax.broadcasted_iota(jnp.int32, (tm, LANES), 1) % HEAD_DIM) < HEAD_DIM // 2

    def rot(xb, c, s):
        partner = jnp.where(first, pltpu.roll(xb, LANES - HEAD_DIM // 2, 1), pltpu.roll(xb, HEAD_DIM // 2, 1))
        return xb * c + partner * s

    for j in range(N_BLOCKS_IN):
        blk = acc[:, LANES * j:LANES * (j + 1)]
        g = j // 3
        if g in (3, 4):
            blk = rot(blk, cr_ref[...], sr_ref[...])
        if g in (7, 8):
            blk = rot(blk, cd_ref[...], sd_ref[...])
        if g in (4, 7):
            blk = blk * (HEAD_DIM ** -0.5)
        p_ref[j] = blk
        if g == 8:
            kn_ref[:, LANES * (j - 24):LANES * (j - 23)] = blk
        if g == 9:
            vn_ref[:, LANES * (j - 27):LANES * (j - 26)] = blk


def _inproj(x, w, cr, sr, cd, sd, tm):
    m = x.shape[0]
    tab = pl.BlockSpec((tm, LANES), lambda i: (i, 0))
    return pl.pallas_call(
        _inproj_kernel,
        grid=(m // tm,),
        in_specs=[pl.BlockSpec((tm, D_MODEL), lambda i: (i, 0)),
                  pl.BlockSpec((D_MODEL, N_BLOCKS_IN * LANES), lambda i: (0, 0)),
                  tab, tab, tab, tab],
        out_specs=[pl.BlockSpec((N_BLOCKS_IN, tm, LANES), lambda i: (0, i, 0)),
                   pl.BlockSpec((tm, GROUP), lambda i: (i, 0)),
                   pl.BlockSpec((tm, GROUP), lambda i: (i, 0))],
        out_shape=[jax.ShapeDtypeStruct((N_BLOCKS_IN, m, LANES), f32),
                   jax.ShapeDtypeStruct((m, GROUP), f32),
                   jax.ShapeDtypeStruct((m, GROUP), f32)],
        compiler_params=_cparams(("parallel",)),
        name="inproj",
    )(x, w, cr, sr, cd, sd)


def _softplus(z):
    return jnp.maximum(z, 0.0) + jnp.log(1.0 + jnp.exp(-jnp.abs(z)))


def _rwkv_prep_kernel(x_ref, xp_ref, r_ref, k_ref, v_ref, rp_ref, kp_ref, vp_ref,
                      mu_rkv_ref, mu_wag_ref, vec_ref, w1_ref, w2_ref, a1_ref, a2_ref, g1_ref, g2_ref, seg_ref,
                      wo_ref, ko_ref, vo_ref, kko_ref, bo_ref, uo_ref, kro_ref, go_ref, bon_ref):
    x = x_ref[...]
    xx = xp_ref[...] - x
    xw = (x + xx * mu_wag_ref[0:1, :]).astype(bf16)
    xa = (x + xx * mu_wag_ref[1:2, :]).astype(bf16)
    xg = (x + xx * mu_wag_ref[2:3, :]).astype(bf16)
    w0, a0, k_k, k_a, r_k = (vec_ref[i:i + 1, :] for i in range(5))

    lw = jnp.tanh(jnp.dot(xw, w1_ref[...], preferred_element_type=f32))
    wl = w0 + jnp.dot(lw.astype(bf16), w2_ref[...], preferred_element_type=f32)
    w_log = -_softplus(-wl) - 0.5
    decay = jnp.exp(-jnp.exp(w_log))
    la = jnp.dot(xa, a1_ref[...], preferred_element_type=f32)
    a = jax.nn.sigmoid(a0 + jnp.dot(la.astype(bf16), a2_ref[...], preferred_element_type=f32))
    lg = jax.nn.sigmoid(jnp.dot(xg, g1_ref[...], preferred_element_type=f32))
    g = jnp.dot(lg.astype(bf16), g2_ref[...], preferred_element_type=f32)

    def shifted(cur_ref, prev_ref, i):
        cur = _cat_blocks(cur_ref)
        return cur + mu_rkv_ref[i:i + 1, :] * (_cat_blocks(prev_ref) - cur)

    r = shifted(r_ref, rp_ref, 0)
    k = shifted(k_ref, kp_ref, 1)
    v = shifted(v_ref, vp_ref, 2)

    seg = seg_ref[...]
    kk = k * k_k
    ss = jnp.dot(kk * kk, seg, precision=HIGHEST, preferred_element_type=f32)
    kk = kk * lax.rsqrt(jnp.maximum(ss, 1e-24))
    k = k * (1.0 + (a - 1.0) * k_a)
    bonus = jnp.dot(r * k * r_k, seg, precision=HIGHEST, preferred_element_type=f32) * v
    b = kk * a
    b_dot_r = jnp.dot(b * r, seg, precision=HIGHEST, preferred_element_type=f32)
    k_dot_r = jnp.dot(k * r, seg, precision=HIGHEST, preferred_element_type=f32)

    _put_blocks(wo_ref, decay)
    _put_blocks(ko_ref, k)
    _put_blocks(vo_ref, v)
    _put_blocks(kko_ref, kk)
    _put_blocks(bo_ref, b)
    _put_blocks(uo_ref, decay * r - kk * b_dot_r)
    _put_blocks(kro_ref, k_dot_r)
    _put_blocks(go_ref, g)
    _put_blocks(bon_ref, bonus)


def _rwkv_prep(x, xp, p, pp, mu_rkv, mu_wag, vec, w1, w2, a1, a2, g1, g2, seg, tm):
    m = x.shape[0]
    row = pl.BlockSpec((tm, D_MODEL), lambda i: (i, 0))

    def grp(gi):
        return pl.BlockSpec((3, tm, LANES), lambda i: (gi, i, 0))

    def full(a):
        return pl.BlockSpec(a.shape, lambda i: (0,) * a.ndim)

    out = jax.ShapeDtypeStruct((3, m, LANES), f32)
    return pl.pallas_call(
        _rwkv_prep_kernel,
        grid=(m // tm,),
        in_specs=[row, row, grp(0), grp(1), grp(2), grp(0), grp(1), grp(2)]
        + [full(a) for a in (mu_rkv, mu_wag, vec, w1, w2, a1, a2, g1, g2, seg)],
        out_specs=[pl.BlockSpec((3, tm, LANES), lambda i: (0, i, 0))] * 9,
        out_shape=[out] * 9,
        compiler_params=_cparams(("parallel",)),
        name="rwkv_prep",
    )(x, xp, p, p, p, pp, pp, pp, mu_rkv, mu_wag, vec, w1, w2, a1, a2, g1, g2, seg)


def _rwkv_scan_kernel(w_ref, k_ref, v_ref, kk_ref, b_ref, u_ref, kr_ref, s0_ref, seg_ref, y_ref, so_ref, s_scr,
                      *, n_steps):
    ti = pl.program_id(1)
    bb = w_ref.shape[2]
    rows = 3 * bb * HEAD_DIM

    @pl.when(ti == 0)
    def _():
        s_scr[...] = s0_ref[...].reshape(rows, LANES)

    if n_steps < y_ref.shape[1]:
        y_ref[...] = jnp.zeros(y_ref.shape, f32)
    eye = ((lax.broadcasted_iota(jnp.int32, (rows, LANES), 0) % HEAD_DIM)
           == (lax.broadcasted_iota(jnp.int32, (rows, LANES), 1) % HEAD_DIM))
    seg = seg_ref[...]

    def rows_of(ref, t):
        return jnp.concatenate([jnp.broadcast_to(ref[c, t, pl.ds(b, 1), :], (HEAD_DIM, LANES))
                                for c in range(3) for b in range(bb)], axis=0)

    def step(t, carry):
        s = s_scr[...]
        p = (s * rows_of(kk_ref, t)).astype(bf16)
        q = (s * rows_of(u_ref, t)).astype(bf16)
        vm = jnp.where(eye, rows_of(v_ref, t), 0.0)
        vhi = vm.astype(bf16)
        vlo = (vm - vhi.astype(f32)).astype(bf16)
        red1 = jnp.dot(jnp.concatenate([p, q], axis=0), seg, preferred_element_type=f32)
        red2 = jnp.dot(jnp.concatenate([vhi, vlo], axis=0), seg, preferred_element_type=f32)
        sa = red1[0:rows]
        vcol = red2[0:rows] + red2[rows:2 * rows]
        ycol = red1[rows:2 * rows] + vcol * rows_of(kr_ref, t)
        s_scr[...] = s * rows_of(w_ref, t) - sa * rows_of(b_ref, t) + vcol * rows_of(k_ref, t)
        ym = jnp.where(eye, ycol, 0.0)
        for c in range(3):
            y_ref[c, t] = jnp.concatenate(
                [jnp.sum(ym[(c * bb + b) * HEAD_DIM:(c * bb + b + 1) * HEAD_DIM], axis=0, keepdims=True)
                 for b in range(bb)], axis=0)
        return carry

    lax.fori_loop(0, n_steps, step, 0)

    @pl.when(ti == pl.num_programs(1) - 1)
    def _():
        so_ref[...] = s_scr[...].reshape(so_ref.shape)


def _rwkv_scan(ops, s0, seg, bb, tc, n_steps):
    _, t, b, _ = ops[0].shape
    blk = pl.BlockSpec((3, tc, bb, LANES), lambda bi, ti: (0, ti, bi, 0))
    st = pl.BlockSpec((3, bb, HEAD_DIM, LANES), lambda bi, ti: (0, bi, 0, 0))
    return pl.pallas_call(
        functools.partial(_rwkv_scan_kernel, n_steps=n_steps),
        grid=(b // bb, t // tc),
        in_specs=[blk] * 7 + [st, pl.BlockSpec((LANES, LANES), lambda bi, ti: (0, 0))],
        out_specs=[blk, st],
        out_shape=[jax.ShapeDtypeStruct((3, t, b, LANES), f32),
                   jax.ShapeDtypeStruct((3, b, HEAD_DIM, LANES), f32)],
        scratch_shapes=[pltpu.VMEM((3 * bb * HEAD_DIM, LANES), f32)],
        compiler_params=_cparams(("parallel", "arbitrary")),
        name="rwkv_scan",
    )(*ops, s0, seg)


def _log_gamma(h):
    return float(np.log(np.float32(1.0) - np.float32(2.0) ** np.float32(-5.0 - min(h, H_RET - 1))))


def _retention_kernel(q_ref, k_ref, v_ref, g_ref, s0_ref, seg_ref, y_ref, so_ref, s_scr, *, n_valid):
    ci = pl.program_id(1)
    rows = q_ref.shape[1]

    @pl.when(ci == 0)
    def _():
        s_scr[...] = s0_ref[0]

    lo = _lane_lo((rows, LANES))
    ii = lax.broadcasted_iota(jnp.int32, (rows, rows), 0)
    jj = lax.broadcasted_iota(jnp.int32, (rows, rows), 1)
    rel = (ii - jj).astype(f32)
    causal = ii >= jj
    pos = lax.broadcasted_iota(jnp.int32, (rows, LANES), 0)
    posf = pos.astype(f32)
    row_ok = pos < n_valid
    sq_r = lax.broadcasted_iota(jnp.int32, (LANES, LANES), 0) < HEAD_DIM
    sq_c = lax.broadcasted_iota(jnp.int32, (LANES, LANES), 1) < HEAD_DIM
    same_head = sq_r == sq_c
    seg = seg_ref[...]

    for c in range(3):
        lg0, lg1 = _log_gamma(2 * c), _log_gamma(2 * c + 1)
        lg_lane = jnp.where(lo, lg0, lg1)
        q = q_ref[c]
        q16 = q.astype(bf16)
        k = jnp.where(row_ok, k_ref[c], 0.0)
        k16 = k.astype(bf16)
        v16 = v_ref[c].astype(bf16)
        o_halves = []
        for half, lg in ((0, lg0), (1, lg1)):
            qm = jnp.where(lo if half == 0 else jnp.logical_not(lo), q, 0.0).astype(bf16)
            s = lax.dot_general(qm, k16, _NT, preferred_element_type=f32)
            dmat = jnp.where(causal, jnp.exp(lg * jnp.maximum(rel, 0.0)), 0.0)
            o_halves.append(jnp.dot((s * dmat).astype(bf16), v16, preferred_element_type=f32))
        o = jnp.where(lo, o_halves[0], o_halves[1])
        state = s_scr[c]
        cross = jnp.dot(q16, state.astype(bf16), preferred_element_type=f32)
        o = o + cross * jnp.exp(lg_lane * (posf + 1.0))
        k_in = (k * jnp.exp(lg_lane * (float(n_valid) - 1.0 - posf))).astype(bf16)
        upd = lax.dot_general(k_in, v16, _TN, preferred_element_type=f32)
        lg_col = jnp.where(sq_c, lg0, lg1)
        s_scr[c] = state * jnp.exp(lg_col * float(n_valid)) + jnp.where(same_head, upd, 0.0)

        mean = jnp.dot(o, seg, precision=HIGHEST, preferred_element_type=f32)
        d = o - mean
        var = jnp.dot(d * d, seg, precision=HIGHEST, preferred_element_type=f32)
        yn = d * lax.rsqrt(var + RET_GN_EPS)
        g = g_ref[c]
        y_ref[c] = g * jax.nn.sigmoid(g) * yn

    @pl.when(ci == pl.num_programs(1) - 1)
    def _():
        so_ref[0] = s_scr[...]


def _retention(p, s0_pairs, seg, n_seq, rows, n_chunks, n_valid, row0):
    def grp(gi):
        return pl.BlockSpec((3, rows, LANES), lambda b, c: (gi, row0 // rows + b * n_chunks + c, 0))

    st = pl.BlockSpec((1, 3, LANES, LANES), lambda b, c: (b, 0, 0, 0))
    return pl.pallas_call(
        functools.partial(_retention_kernel, n_valid=n_valid),
        grid=(n_seq, n_chunks),
        in_specs=[grp(3), grp(4), grp(5), grp(6), st, pl.BlockSpec(seg.shape, lambda b, c: (0, 0))],
        out_specs=[pl.BlockSpec((3, rows, LANES), lambda b, c: (0, b * n_chunks + c, 0)), st],
        out_shape=[jax.ShapeDtypeStruct((3, n_seq * n_chunks * rows, LANES), f32),
                   jax.ShapeDtypeStruct((n_seq, 3, LANES, LANES), f32)],
        scratch_shapes=[pltpu.VMEM((3, LANES, LANES), f32)],
        compiler_params=_cparams(("parallel", "arbitrary")),
        name="retention",
    )(p, p, p, p, s0_pairs, seg)


def _dil_prompt_kernel(q_ref, k_ref, v_ref, y_ref, o_scr, l_scr):
    i = pl.program_id(1)
    n_steps = pl.num_programs(1)
    blk = DIL_BLK
    lo = _lane_lo((blk, LANES))
    ii = lax.broadcasted_iota(jnp.int32, (blk, 2 * blk), 0)
    jj = lax.broadcasted_iota(jnp.int32, (blk, 2 * blk), 1)
    band = jnp.logical_and(jj >= ii, jj <= ii + blk)

    for p, dil in enumerate(DILATIONS):
        nb = q_ref.shape[1] // (dil * blk)
        r = i // nb
        cb = i % nb
        start_q = r + dil * blk * cb
        start_p = r + dil * blk * jnp.maximum(cb - 1, 0)
        valid = jnp.logical_and(band, jnp.logical_or(cb > 0, jj >= blk))

        def rows(ref, c, start):
            if dil == 1:
                return ref[c, pl.ds(start, blk), :]
            return ref[c, pl.ds(start, blk, stride=dil), :]

        for c in range(3):
            q = rows(q_ref, c, start_q)
            kwin = jnp.concatenate([rows(k_ref, c, start_p), rows(k_ref, c, start_q)], axis=0).astype(bf16)
            vwin = jnp.concatenate([rows(v_ref, c, start_p), rows(v_ref, c, start_q)], axis=0).astype(bf16)
            outs, lses = [], []
            for half in range(2):
                qm = jnp.where(lo if half == 0 else jnp.logical_not(lo), q, 0.0).astype(bf16)
                s = lax.dot_general(qm, kwin, _NT, preferred_element_type=f32)
                s = jnp.where(valid, s, -jnp.inf)
                mx = jnp.max(s, axis=-1, keepdims=True)
                lse = mx + jnp.log(jnp.sum(jnp.exp(s - mx), axis=-1, keepdims=True))
                prob = jnp.exp(s - lse).astype(bf16)
                outs.append(jnp.dot(prob, vwin, preferred_element_type=f32))
                lses.append(lse)
            o_pair = jnp.where(lo, outs[0], outs[1])
            l_pair = jnp.where(lo, lses[0], lses[1])
            if dil == 1:
                o_scr[p, c, pl.ds(start_q, blk), :] = o_pair
                l_scr[p, c, pl.ds(start_q, blk), :] = l_pair
            else:
                o_scr[p, c, pl.ds(start_q, blk, stride=dil), :] = o_pair
                l_scr[p, c, pl.ds(start_q, blk, stride=dil), :] = l_pair

    @pl.when(i == n_steps - 1)
    def _():
        def merge(j, carry):
            sl = pl.ds(pl.multiple_of(j * blk, blk), blk)
            for c in range(3):
                ls = [l_scr[p, c, sl, :] for p in range(3)]
                mx = jnp.maximum(jnp.maximum(ls[0], ls[1]), ls[2])
                es = [jnp.exp(l - mx) for l in ls]
                tot = es[0] + es[1] + es[2]
                acc = (es[0] / tot) * o_scr[0, c, sl, :]
                acc = acc + (es[1] / tot) * o_scr[1, c, sl, :]
                acc = acc + (es[2] / tot) * o_scr[2, c, sl, :]
                y_ref[c, sl, :] = acc
            return carry

        lax.fori_loop(0, q_ref.shape[1] // blk, merge, 0)


def _dil_prompt(p, n_seq, t):
    def grp(gi):
        return pl.BlockSpec((3, t, LANES), lambda b, i: (gi, b, 0))

    n_steps = t // DIL_BLK
    return pl.pallas_call(
        _dil_prompt_kernel,
        grid=(n_seq, n_steps),
        in_specs=[grp(7), grp(8), grp(9)],
        out_specs=pl.BlockSpec((3, t, LANES), lambda b, i: (0, b, 0)),
        out_shape=jax.ShapeDtypeStruct((3, n_seq * t, LANES), f32),
        scratch_shapes=[pltpu.VMEM((3, 3, t, LANES), f32), pltpu.VMEM((3, 3, t, LANES), f32)],
        compiler_params=_cparams(("parallel", "arbitrary")),
        name="dil_prompt",
    )(p, p, p)


N_NEW = 4
WIN_KEYS = 2 * DIL_BLK


def _dil_sample_kernel(q_ref, kn_ref, vn_ref, kc_ref, vc_ref, y_ref, ko_ref, vo_ref, kall, vall, *, n_buf):
    n_rows = 3 * n_buf
    tail0 = (n_rows + 3 * N_NEW) // 8 * 8

    for new_ref, cache_ref, out_ref, scr in ((kn_ref, kc_ref, ko_ref, kall), (vn_ref, vc_ref, vo_ref, vall)):
        @pl.when(pl.program_id(0) == 0)
        def _(scr=scr):
            scr[pl.ds(tail0, scr.shape[0] - tail0), :] = jnp.zeros((scr.shape[0] - tail0, LANES), f32)

        scr[pl.ds(0, n_rows), :] = cache_ref[0]
        for t in range(N_NEW):
            for c in range(3):
                scr[pl.ds(n_rows + 3 * t + c, 1), :] = new_ref[c, pl.ds(t, 1), :]
        out_ref[0] = scr[pl.ds(3 * N_NEW, n_rows), :]

    lo8 = _lane_lo((8, LANES))
    sub8 = lax.broadcasted_iota(jnp.int32, (8, LANES), 0)
    key_ok = lax.broadcasted_iota(jnp.int32, (8, WIN_KEYS), 1) <= DIL_BLK
    y_ref[...] = jnp.zeros(y_ref.shape, f32)

    for t in range(N_NEW):
        for c in range(3):
            q_row = q_ref[c, pl.ds(t, 1), :]
            qm = jnp.where(jnp.logical_or(jnp.logical_and(sub8 == 0, lo8),
                                          jnp.logical_and(sub8 == 1, jnp.logical_not(lo8))), q_row, 0.0).astype(bf16)
            outs, lses = [], []
            for dil in DILATIONS:
                start = 3 * (n_buf + t - dil * DIL_BLK) + c
                kwin = kall[pl.ds(start, WIN_KEYS, stride=3 * dil), :].astype(bf16)
                vwin = vall[pl.ds(start, WIN_KEYS, stride=3 * dil), :].astype(bf16)
                s = lax.dot_general(qm, kwin, _NT, preferred_element_type=f32)
                s = jnp.where(key_ok, s, -jnp.inf)
                mx = jnp.max(s, axis=-1, keepdims=True)
                lse = mx + jnp.log(jnp.sum(jnp.exp(s - mx), axis=-1, keepdims=True))
                prob = jnp.exp(s - lse).astype(bf16)
                outs.append(jnp.dot(prob, vwin, preferred_element_type=f32))
                lses.append(lse)
            mx = jnp.maximum(jnp.maximum(lses[0], lses[1]), lses[2])
            es = [jnp.exp(l - mx) for l in lses]
            tot = es[0] + es[1] + es[2]
            acc = (es[0] / tot) * outs[0] + (es[1] / tot) * outs[1] + (es[2] / tot) * outs[2]
            y_ref[c, pl.ds(t, 1), :] = jnp.where(lo8[0:1, :], acc[0:1, :], acc[1:2, :])


def _dil_sample(p, cache_k, cache_v, n_seq, row0, n_buf):
    n_rows = 3 * n_buf
    scr_rows = -(-(3 * (n_buf + N_NEW + DILATIONS[-1] * (WIN_KEYS - DIL_BLK)) + 8) // 8) * 8

    def grp(gi):
        return pl.BlockSpec((3, DEC_PAD, LANES), lambda b: (gi, row0 // DEC_PAD + b, 0))

    buf = pl.BlockSpec((1, n_rows, LANES), lambda b: (b, 0, 0))
    return pl.pallas_call(
        functools.partial(_dil_sample_kernel, n_buf=n_buf),
        grid=(n_seq,),
        in_specs=[grp(7), grp(8), grp(9), buf, buf],
        out_specs=[pl.BlockSpec((3, DEC_PAD, LANES), lambda b: (0, b, 0)), buf, buf],
        out_shape=[jax.ShapeDtypeStruct((3, n_seq * DEC_PAD, LANES), f32),
                   jax.ShapeDtypeStruct((n_seq, n_rows, LANES), f32),
                   jax.ShapeDtypeStruct((n_seq, n_rows, LANES), f32)],
        scratch_shapes=[pltpu.VMEM((scr_rows, LANES), f32), pltpu.VMEM((scr_rows, LANES), f32)],
        compiler_params=_cparams(("arbitrary",)),
        name="dil_sample",
    )(p, p, p, cache_k, cache_v)


def _layer_norm(z, gamma, beta):
    mu = jnp.mean(z, axis=-1, keepdims=True)
    d = z - mu
    var = jnp.mean(d * d, axis=-1, keepdims=True)
    return d * lax.rsqrt(var + LN_EPS) * gamma + beta


def _merge_kernel(ya_ref, bon_ref, g_ref, yb_ref, yc_ref, x_ref, lnx_ref, seg_ref, wo_ref, ln_ref, o_ref):
    yr = _cat_blocks(ya_ref)
    seg = seg_ref[...]
    mean = jnp.dot(yr, seg, precision=HIGHEST, preferred_element_type=f32)
    d = yr - mean
    var = jnp.dot(d * d, seg, precision=HIGHEST, preferred_element_type=f32)
    ya = d * lax.rsqrt(var + RWKV_GN_EPS) * lnx_ref[0:1, :] + lnx_ref[1:2, :]
    ya = (ya + _cat_blocks(bon_ref)) * _cat_blocks(g_ref)
    ycat = jnp.concatenate([ya, _cat_blocks(yb_ref), _cat_blocks(yc_ref)], axis=1).astype(bf16)
    h = jnp.dot(ycat, wo_ref[...], preferred_element_type=f32)
    o_ref[...] = _layer_norm(DEEPNORM_ALPHA * x_ref[...] + h, ln_ref[0:1, :], ln_ref[1:2, :])


def _merge(ya, bonus, g, yb, yc, x, lnx, seg, wo, ln, tm):
    m = x.shape[0]
    blk = pl.BlockSpec((3, tm, LANES), lambda i: (0, i, 0))

    def full(a):
        return pl.BlockSpec(a.shape, lambda i: (0,) * a.ndim)

    return pl.pallas_call(
        _merge_kernel,
        grid=(m // tm,),
        in_specs=[blk] * 5 + [pl.BlockSpec((tm, D_MODEL), lambda i: (i, 0)), full(lnx), full(seg), full(wo), full(ln)],
        out_specs=pl.BlockSpec((tm, D_MODEL), lambda i: (i, 0)),
        out_shape=jax.ShapeDtypeStruct((m, D_MODEL), f32),
        compiler_params=_cparams(("parallel",)),
        name="mixer_merge",
    )(ya, bonus, g, yb, yc, x, lnx, seg, wo, ln)


EXPERT_BLOCK = 8


def _first_max(x, idx, axes, sentinel):
    mx = x
    for ax in axes:
        mx = jnp.max(mx, axis=ax, keepdims=True)
    am = jnp.where(x == mx, idx, sentinel)
    for ax in axes:
        am = jnp.min(am, axis=ax, keepdims=True)
    return mx, am


def _route(logits_t, bias_col):
    n_tok = logits_t.shape[1]
    per = N_EXPERTS // N_ROUTE_GROUPS
    scores = jax.nn.sigmoid(logits_t)
    shp = (N_ROUTE_GROUPS, per, n_tok)
    s3 = scores.reshape(shp)
    b3 = (scores + bias_col).reshape(shp)
    sub = lax.broadcasted_iota(jnp.int32, shp, 1)
    grp = lax.broadcasted_iota(jnp.int32, shp, 0)
    eid = grp * per + sub
    neg = -jnp.inf

    m1, i1 = _first_max(b3, sub, (1,), per)
    m2, _ = _first_max(jnp.where(sub == i1, neg, b3), sub, (1,), per)
    gscore = m1 + m2
    gid = lax.broadcasted_iota(jnp.int32, gscore.shape, 0)
    gsel = jnp.zeros(gscore.shape, jnp.bool_)
    for _ in range(TOPK_GROUPS):
        _, gi = _first_max(gscore, gid, (0,), N_ROUTE_GROUPS)
        hit = gid == gi
        gsel = jnp.logical_or(gsel, hit)
        gscore = jnp.where(hit, neg, gscore)
    masked = jnp.where(gsel, b3, neg)
    esel = jnp.zeros(shp, jnp.bool_)
    for _ in range(TOP_K):
        _, ei = _first_max(masked, eid, (1, 0), N_EXPERTS)
        hit = eid == ei
        esel = jnp.logical_or(esel, hit)
        masked = jnp.where(hit, neg, masked)
    top_s = jnp.where(esel, s3, 0.0)
    denom = jnp.sum(jnp.sum(top_s, axis=1, keepdims=True), axis=0, keepdims=True)
    return (top_s / denom * ROUTE_SCALE).reshape(N_EXPERTS, n_tok)


def _moe_kernel(x_ref, wr_ref, rb_ref, wg_ref, wu_ref, wd_ref, ex_ref, wsg_ref, wsu_ref, wsd_ref, ln_ref,
                o_ref, x16_scr, chi_scr, clo_scr, acc_scr):
    j = pl.program_id(1)

    @pl.when(j == 0)
    def _():
        x16 = x_ref[...].astype(bf16)
        x16_scr[...] = x16
        logits_t = lax.dot_general(wr_ref[...], x16, _NT, preferred_element_type=f32)
        comb_t = _route(logits_t, rb_ref[...])
        comb_t = jnp.concatenate([comb_t, jnp.zeros_like(comb_t)], axis=0)
        comb = comb_t.T
        hi = comb.astype(bf16)
        chi_scr[...] = hi
        clo_scr[...] = (comb - hi.astype(f32)).astype(bf16)
        sg = jnp.dot(x16, wsg_ref[...], preferred_element_type=f32)
        su = jnp.dot(x16, wsu_ref[...], preferred_element_type=f32)
        hs = (sg * jax.nn.sigmoid(sg) * su).astype(bf16)
        acc_scr[...] = jnp.dot(hs, wsd_ref[...], preferred_element_type=f32)

    x16 = x16_scr[...]
    gate = jnp.dot(x16, wg_ref[...], preferred_element_type=f32)
    up = jnp.dot(x16, wu_ref[...], preferred_element_type=f32)
    ex = ex_ref[...]
    cexp = (jnp.dot(chi_scr[...], ex, preferred_element_type=f32)
            + jnp.dot(clo_scr[...], ex, preferred_element_type=f32))
    h = (gate * jax.nn.sigmoid(gate) * up * cexp).astype(bf16)
    acc_scr[...] += jnp.dot(h, wd_ref[...], preferred_element_type=f32)

    @pl.when(j == pl.num_programs(1) - 1)
    def _():
        o_ref[...] = _layer_norm(DEEPNORM_ALPHA * x_ref[...] + acc_scr[...], ln_ref[0:1, :], ln_ref[1:2, :])


def _moe(x, wr_t, rb_col, wg, wu, wd, expand, wsg, wsu, wsd, ln, tm):
    m = x.shape[0]
    nb = EXPERT_BLOCK * D_EXPERT

    def full(a):
        return pl.BlockSpec(a.shape, lambda i, j: (0,) * a.ndim)

    return pl.pallas_call(
        _moe_kernel,
        grid=(m // tm, N_EXPERTS // EXPERT_BLOCK),
        in_specs=[pl.BlockSpec((tm, D_MODEL), lambda i, j: (i, 0)), full(wr_t), full(rb_col),
                  pl.BlockSpec((D_MODEL, nb), lambda i, j: (0, j)),
                  pl.BlockSpec((D_MODEL, nb), lambda i, j: (0, j)),
                  pl.BlockSpec((nb, D_MODEL), lambda i, j: (j, 0)),
                  pl.BlockSpec((LANES, nb), lambda i, j: (0, j)),
                  full(wsg), full(wsu), full(wsd), full(ln)],
        out_specs=pl.BlockSpec((tm, D_MODEL), lambda i, j: (i, 0)),
        out_shape=jax.ShapeDtypeStruct((m, D_MODEL), f32),
        scratch_shapes=[pltpu.VMEM((tm, D_MODEL), bf16), pltpu.VMEM((tm, LANES), bf16),
                        pltpu.VMEM((tm, LANES), bf16), pltpu.VMEM((tm, D_MODEL), f32)],
        compiler_params=_cparams(("parallel", "arbitrary")),
        name="moe",
    )(x, wr_t, rb_col, wg, wu, wd, expand, wsg, wsu, wsd, ln)


def _pad_heads(a, n_heads):
    return jnp.pad(a, [(0, 0)] * (a.ndim - 1) + [(0, GROUP - n_heads * HEAD_DIM)])


def _pad_w_in(w):
    widths = [C_RWKV] * 3 + [C_RET] * 4 + [C_DIL] * 3
    parts, off = [], 0
    for wd in widths:
        parts.append(jnp.pad(w[:, off:off + wd], ((0, 0), (0, GROUP - wd))))
        off += wd
    return jnp.concatenate(parts, axis=1).astype(bf16)


def _rope_tables(pos, inv_freq):
    ang = pos.astype(f32)[:, None] * inv_freq[None, :]
    cos, sin = jnp.cos(ang), jnp.sin(ang)
    return jnp.tile(jnp.concatenate([cos, cos], axis=1), (1, 2)), jnp.tile(jnp.concatenate([-sin, sin], axis=1), (1, 2))


def _segment_matrix(width, scale):
    idx = np.arange(width) // HEAD_DIM
    return jnp.asarray((idx[:, None] == idx[None, :]).astype(np.float32) * scale)


def _pairs_from_heads(s):
    b = s.shape[0]
    s6 = jnp.pad(s, ((0, 0), (0, 1), (0, 0), (0, 0))).reshape(b, 3, 2, HEAD_DIM, HEAD_DIM)
    z = jnp.zeros((b, 3, HEAD_DIM, HEAD_DIM), s.dtype)
    top = jnp.concatenate([s6[:, :, 0], z], axis=-1)
    bot = jnp.concatenate([z, s6[:, :, 1]], axis=-1)
    return jnp.concatenate([top, bot], axis=-2)


def _heads_from_pairs(sp):
    b = sp.shape[0]
    h0 = sp[:, :, :HEAD_DIM, :HEAD_DIM]
    h1 = sp[:, :, HEAD_DIM:, HEAD_DIM:]
    return jnp.stack([h0, h1], axis=2).reshape(b, 6, HEAD_DIM, HEAD_DIM)[:, :H_RET]


def _rwkv_pairs_from_heads(s):
    b = s.shape[0]
    s6 = jnp.pad(s, ((0, 0), (0, 1), (0, 0), (0, 0))).reshape(b, 3, 2, HEAD_DIM, HEAD_DIM)
    return jnp.transpose(s6, (1, 0, 3, 2, 4)).reshape(3, b, HEAD_DIM, LANES)


def _rwkv_heads_from_pairs(sp):
    b = sp.shape[1]
    s6 = jnp.transpose(sp.reshape(3, b, HEAD_DIM, 2, HEAD_DIM), (1, 0, 3, 2, 4))
    return s6.reshape(b, 6, HEAD_DIM, HEAD_DIM)[:, :H_RWKV]


def kernel(x_prompt, x_sample, state_rwkv, state_shift, state_ret, cache_k_win, cache_v_win, w_in, mu_rkv, mu_wag, w0, w1, w2, a0, a1, a2, g1, g2, k_k, k_a, r_k, lnx_g, lnx_b, w_out, ln1_g, ln1_b, w_router, router_bias, w_gate, w_up, w_down, ws_gate, ws_up, ws_down, ln2_g, ln2_b):
    bp, tp, _ = x_prompt.shape
    bs, ts, _ = x_sample.shape
    n_buf = cache_k_win.shape[2]
    depth = w_in.shape[0]
    assert ts == N_NEW and n_buf == DILATIONS[-1] * DIL_BLK and tp % (DILATIONS[-1] * DIL_BLK) == 0
    mp = bp * tp
    ms = bs * DEC_PAD
    m = mp + ms
    tm = 256
    tm_moe = 512
    assert m % tm_moe == 0 and bs % 8 == 0

    pos = jnp.concatenate([jnp.tile(jnp.arange(tp), bp), jnp.tile(PAST_LEN + jnp.arange(DEC_PAD), bs)])
    inv_ret = 1.0 / (10000.0 ** jnp.linspace(0.0, 1.0, HEAD_DIM // 2, dtype=f32))
    inv_dil = ROPE_THETA ** (-jnp.arange(0, HEAD_DIM, 2, dtype=f32) / HEAD_DIM)
    cr, sr = _rope_tables(pos, inv_ret)
    cd, sd = _rope_tables(pos, inv_dil)
    zero_tab = jnp.zeros((bs, LANES), f32)

    seg_sum = _segment_matrix(GROUP, 1.0)
    seg_mean = _segment_matrix(GROUP, 1.0 / HEAD_DIM)
    seg_mean_pair = _segment_matrix(LANES, 1.0 / HEAD_DIM)
    seg_pair_bf16 = _segment_matrix(LANES, 1.0).astype(bf16)
    expand = jnp.asarray((np.arange(LANES)[:, None] == (np.arange(N_EXPERTS * D_EXPERT) // D_EXPERT)[None, :])
                         .astype(np.float32)).astype(bf16)

    xs_pad = jnp.pad(x_sample, ((0, 0), (0, DEC_PAD - ts), (0, 0)))
    x = jnp.concatenate([x_prompt.reshape(mp, D_MODEL), xs_pad.reshape(ms, D_MODEL)], axis=0)

    outs_p = [[] for _ in range(5)]
    outs_s = [[] for _ in range(5)]
    for l in range(depth):
        xp3 = x[:mp].reshape(bp, tp, D_MODEL)
        xs3 = x[mp:].reshape(bs, DEC_PAD, D_MODEL)
        x_last_s = state_shift[l]
        outs_p[1].append(xp3[:, -1])
        outs_s[1].append(xs3[:, ts - 1])
        x_prev = jnp.concatenate([
            jnp.concatenate([jnp.zeros((bp, 1, D_MODEL), f32), xp3[:, :-1]], axis=1).reshape(mp, D_MODEL),
            jnp.concatenate([x_last_s[:, None], xs3[:, :-1]], axis=1).reshape(ms, D_MODEL)], axis=0)

        w_in_p = _pad_w_in(w_in[l])
        p, k_new, v_new = _inproj(x, w_in_p, cr, sr, cd, sd, tm)
        p_last, _, _ = _inproj(x_last_s, w_in_p, zero_tab, zero_tab, zero_tab, zero_tab, min(bs, 128))

        pr = p[:9]
        pr_p = pr[:, :mp].reshape(9, bp, tp, LANES)
        pr_s = pr[:, mp:].reshape(9, bs, DEC_PAD, LANES)
        pp = jnp.concatenate([
            jnp.concatenate([jnp.zeros((9, bp, 1, LANES), f32), pr_p[:, :, :-1]], axis=2).reshape(9, mp, LANES),
            jnp.concatenate([p_last[:9][:, :, None], pr_s[:, :, :-1]], axis=2).reshape(9, ms, LANES)], axis=1)

        vec = jnp.stack([_pad_heads(v, H_RWKV) for v in (w0[l], a0[l], k_k[l], k_a[l], r_k[l].reshape(-1))])
        prep = _rwkv_prep(x, x_prev, p, pp, _pad_heads(mu_rkv[l], H_RWKV), mu_wag[l], vec,
                          w1[l].astype(bf16), _pad_heads(w2[l], H_RWKV).astype(bf16),
                          a1[l].astype(bf16), _pad_heads(a2[l], H_RWKV).astype(bf16),
                          g1[l].astype(bf16), _pad_heads(g2[l], H_RWKV).astype(bf16), seg_sum, tm)
        scan_ops, gate_a, bonus = prep[:7], prep[7], prep[8]

        ops_p = [jnp.swapaxes(a[:, :mp].reshape(3, bp, tp, LANES), 1, 2) for a in scan_ops]
        ops_s = [jnp.swapaxes(a[:, mp:].reshape(3, bs, DEC_PAD, LANES), 1, 2) for a in scan_ops]
        ya_p, s_rwkv_p = _rwkv_scan(ops_p, jnp.zeros((3, bp, HEAD_DIM, LANES), f32), seg_pair_bf16,
                                    bp if bp < 8 else 8, 64, 64)
        ya_s, s_rwkv_s = _rwkv_scan(ops_s, _rwkv_pairs_from_heads(state_rwkv[l]), seg_pair_bf16, 8, DEC_PAD, ts)
        ya = jnp.concatenate([jnp.swapaxes(ya_p, 1, 2).reshape(3, mp, LANES),
                              jnp.swapaxes(ya_s, 1, 2).reshape(3, ms, LANES)], axis=1)
        outs_p[0].append(_rwkv_heads_from_pairs(s_rwkv_p))
        outs_s[0].append(_rwkv_heads_from_pairs(s_rwkv_s))

        yb_p, s_ret_p = _retention(p, jnp.zeros((bp, 3, LANES, LANES), f32), seg_mean_pair,
                                   bp, RET_CHUNK, tp // RET_CHUNK, RET_CHUNK, 0)
        yb_s, s_ret_s = _retention(p, _pairs_from_heads(state_ret[l]), seg_mean_pair, bs, DEC_PAD, 1, ts, mp)
        yb = jnp.concatenate([yb_p, yb_s], axis=1)
        outs_p[2].append(_heads_from_pairs(s_ret_p))
        outs_s[2].append(_heads_from_pairs(s_ret_s))

        yc_p = _dil_prompt(p, bp, tp)
        yc_s, k_win, v_win = _dil_sample(p, cache_k_win[l].reshape(bs, 3 * n_buf, LANES),
                                         cache_v_win[l].reshape(bs, 3 * n_buf, LANES), bs, mp, n_buf)
        yc = jnp.concatenate([yc_p, yc_s], axis=1)
        keep = min(n_buf, tp)
        outs_p[3].append(k_new[:mp].reshape(bp, tp, H_DIL, HEAD_DIM)[:, tp - keep:])
        outs_p[4].append(v_new[:mp].reshape(bp, tp, H_DIL, HEAD_DIM)[:, tp - keep:])
        outs_s[3].append(k_win.reshape(bs, n_buf, H_DIL, HEAD_DIM))
        outs_s[4].append(v_win.reshape(bs, n_buf, H_DIL, HEAD_DIM))

        w_out_p = jnp.concatenate([
            jnp.pad(w_out[l][:C_RWKV], ((0, GROUP - C_RWKV), (0, 0))),
            jnp.pad(w_out[l][C_RWKV:C_RWKV + C_RET], ((0, GROUP - C_RET), (0, 0))),
            w_out[l][C_RWKV + C_RET:]], axis=0).astype(bf16)
        lnx = jnp.stack([_pad_heads(lnx_g[l], H_RWKV), _pad_heads(lnx_b[l], H_RWKV)])
        x = _merge(ya, bonus, gate_a, yb, yc, x, lnx, seg_mean, w_out_p,
                   jnp.stack([ln1_g[l], ln1_b[l]]), tm)

        wg = jnp.transpose(w_gate[l], (1, 0, 2)).reshape(D_MODEL, N_EXPERTS * D_EXPERT).astype(bf16)
        wu = jnp.transpose(w_up[l], (1, 0, 2)).reshape(D_MODEL, N_EXPERTS * D_EXPERT).astype(bf16)
        wd = w_down[l].reshape(N_EXPERTS * D_EXPERT, D_MODEL).astype(bf16)
        x = _moe(x, w_router[l].T.astype(bf16), router_bias[l].reshape(N_EXPERTS, 1), wg, wu, wd, expand,
                 ws_gate[l].astype(bf16), ws_up[l].astype(bf16), ws_down[l].astype(bf16),
                 jnp.stack([ln2_g[l], ln2_b[l]]), tm_moe)

    y_prompt = x[:mp].reshape(bp, tp, D_MODEL)
    y_sample = x[mp:].reshape(bs, DEC_PAD, D_MODEL)[:, :ts]
    return (y_prompt, y_sample) + tuple(jnp.stack(o) for o in outs_p) + tuple(jnp.stack(o) for o in outs_s)
```

```python
import functools
import math

import numpy as np
import jax
import jax.numpy as jnp
from jax import lax
from jax.experimental import pallas as pl
from jax.experimental.pallas import tpu as pltpu

f32 = jnp.float32
bf16 = jnp.bfloat16

D_MODEL = 1024
HEAD_DIM = 64
LANES = 128
H_RWKV = 5
H_RET = 5
H_DIL = 6
GROUP = 384
N_GROUPS_IN = 10
N_BLOCKS_IN = 3 * N_GROUPS_IN
C_RWKV = H_RWKV * HEAD_DIM
C_RET = H_RET * HEAD_DIM
C_DIL = H_DIL * HEAD_DIM
PAST_LEN = 2048
DEC_PAD = 8
RET_CHUNK = 128
RWKV_GN_EPS = 64e-5
RET_GN_EPS = 1e-5
DILATIONS = (1, 4, 16)
DIL_BLK = 128
ROPE_THETA = 10000.0
N_EXPERTS = 64
D_EXPERT = 128
TOP_K = 8
N_ROUTE_GROUPS = 8
TOPK_GROUPS = 4
ROUTE_SCALE = 2.5
LN_EPS = 1e-5
DEPTH = 2
DEEPNORM_ALPHA = (2 * DEPTH) ** 0.25
VMEM_LIMIT = 56 * 1024 * 1024
HIGHEST = lax.Precision.HIGHEST

_NT = (((1,), (1,)), ((), ()))
_TN = (((0,), (0,)), ((), ()))


def _cparams(sem):
    return pltpu.CompilerParams(dimension_semantics=sem, vmem_limit_bytes=VMEM_LIMIT)


def _lane_lo(shape):
    return (lax.broadcasted_iota(jnp.int32, shape, len(shape) - 1) % LANES) < HEAD_DIM


def _cat_blocks(ref):
    return jnp.concatenate([ref[0], ref[1], ref[2]], axis=1)


def _put_blocks(ref, val):
    for c in range(3):
        ref[c] = val[:, LANES * c:LANES * (c + 1)]


def _inproj_kernel(x_ref, w_ref, cr_ref, sr_ref, cd_ref, sd_ref, p_ref, kn_ref, vn_ref):
    acc = jnp.dot(x_ref[...].astype(bf16), w_ref[...], preferred_element_type=f32)
    tm = acc.shape[0]
    first = (lax.broadcasted_iota(jnp.int32, (tm, LANES), 1) % HEAD_DIM) < HEAD_DIM // 2

    def rot(xb, c, s):
        partner = jnp.where(first, pltpu.roll(xb, LANES - HEAD_DIM // 2, 1), pltpu.roll(xb, HEAD_DIM // 2, 1))
        return xb * c + partner * s

    for j in range(N_BLOCKS_IN):
        blk = acc[:, LANES * j:LANES * (j + 1)]
        g = j // 3
        if g in (3, 4):
            blk = rot(blk, cr_ref[...], sr_ref[...])
        if g in (7, 8):
            blk = rot(blk, cd_ref[...], sd_ref[...])
        if g in (4, 7):
            blk = blk * (HEAD_DIM ** -0.5)
        p_ref[j] = blk
        if g == 8:
            kn_ref[:, LANES * (j - 24):LANES * (j - 23)] = blk
        if g == 9:
            vn_ref[:, LANES * (j - 27):LANES * (j - 26)] = blk


def _inproj(x, w, cr, sr, cd, sd, tm):
    m = x.shape[0]
    tab = pl.BlockSpec((tm, LANES), lambda i: (i, 0))
    return pl.pallas_call(
        _inproj_kernel,
        grid=(m // tm,),
        in_specs=[pl.BlockSpec((tm, D_MODEL), lambda i: (i, 0)),
                  pl.BlockSpec((D_MODEL, N_BLOCKS_IN * LANES), lambda i: (0, 0)),
                  tab, tab, tab, tab],
        out_specs=[pl.BlockSpec((N_BLOCKS_IN, tm, LANES), lambda i: (0, i, 0)),
                   pl.BlockSpec((tm, GROUP), lambda i: (i, 0)),
                   pl.BlockSpec((tm, GROUP), lambda i: (i, 0))],
        out_shape=[jax.ShapeDtypeStruct((N_BLOCKS_IN, m, LANES), f32),
                   jax.ShapeDtypeStruct((m, GROUP), f32),
                   jax.ShapeDtypeStruct((m, GROUP), f32)],
        compiler_params=_cparams(("parallel",)),
        name="inproj",
    )(x, w, cr, sr, cd, sd)


def _softplus(z):
    return jnp.maximum(z, 0.0) + jnp.log(1.0 + jnp.exp(-jnp.abs(z)))


def _rwkv_prep_core(x, x_prev, rkv, rkv_prev,
                    mu_rkv_ref, mu_wag_ref, vec_ref, w1_ref, w2_ref, a1_ref, a2_ref, g1_ref, g2_ref, seg_ref,
                    wo_ref, ko_ref, vo_ref, kko_ref, bo_ref, uo_ref, kro_ref, go_ref, bon_ref):
    xx = x_prev - x
    xw = (x + xx * mu_wag_ref[0:1, :]).astype(bf16)
    xa = (x + xx * mu_wag_ref[1:2, :]).astype(bf16)
    xg = (x + xx * mu_wag_ref[2:3, :]).astype(bf16)
    w0, a0, k_k, k_a, r_k = (vec_ref[i:i + 1, :] for i in range(5))

    lw = jnp.tanh(jnp.dot(xw, w1_ref[...], preferred_element_type=f32))
    wl = w0 + jnp.dot(lw.astype(bf16), w2_ref[...], preferred_element_type=f32)
    w_log = -_softplus(-wl) - 0.5
    decay = jnp.exp(-jnp.exp(w_log))
    la = jnp.dot(xa, a1_ref[...], preferred_element_type=f32)
    a = jax.nn.sigmoid(a0 + jnp.dot(la.astype(bf16), a2_ref[...], preferred_element_type=f32))
    lg = jax.nn.sigmoid(jnp.dot(xg, g1_ref[...], preferred_element_type=f32))
    g = jnp.dot(lg.astype(bf16), g2_ref[...], preferred_element_type=f32)

    def shifted(i):
        cur = rkv[:, GROUP * i:GROUP * (i + 1)]
        return cur + mu_rkv_ref[i:i + 1, :] * (rkv_prev[:, GROUP * i:GROUP * (i + 1)] - cur)

    r = shifted(0)
    k = shifted(1)
    v = shifted(2)

    seg = seg_ref[...]
    kk = k * k_k
    ss = jnp.dot(kk * kk, seg, precision=HIGHEST, preferred_element_type=f32)
    kk = kk * lax.rsqrt(jnp.maximum(ss, 1e-24))
    k = k * (1.0 + (a - 1.0) * k_a)
    bonus = jnp.dot(r * k * r_k, seg, precision=HIGHEST, preferred_element_type=f32) * v
    b = kk * a
    b_dot_r = jnp.dot(b * r, seg, precision=HIGHEST, preferred_element_type=f32)
    k_dot_r = jnp.dot(k * r, seg, precision=HIGHEST, preferred_element_type=f32)

    _put_blocks(wo_ref, decay)
    _put_blocks(ko_ref, k)
    _put_blocks(vo_ref, v)
    _put_blocks(kko_ref, kk)
    _put_blocks(bo_ref, b)
    _put_blocks(uo_ref, decay * r - kk * b_dot_r)
    _put_blocks(kro_ref, k_dot_r)
    _put_blocks(go_ref, g)
    _put_blocks(bon_ref, bonus)


def _rwkv_prep_seq_kernel(x_ref, r_ref, k_ref, v_ref, *rest):
    params, outs, (x_carry, rkv_carry) = rest[:10], rest[10:19], rest[19:]
    ti = pl.program_id(1)

    @pl.when(ti == 0)
    def _():
        x_carry[...] = jnp.zeros(x_carry.shape, f32)
        rkv_carry[...] = jnp.zeros(rkv_carry.shape, f32)

    x = x_ref[...]
    rkv = jnp.concatenate([_cat_blocks(r_ref), _cat_blocks(k_ref), _cat_blocks(v_ref)], axis=1)
    tm = x.shape[0]

    def prev_rows(cur, carry):
        first = lax.broadcasted_iota(jnp.int32, cur.shape, 0) == 0
        prev = jnp.where(first, carry[7:8, :], pltpu.roll(cur, 1, 0))
        carry[...] = cur[tm - 8:tm]
        return prev

    _rwkv_prep_core(x, prev_rows(x, x_carry), rkv, prev_rows(rkv, rkv_carry), *params, *outs)


def _rwkv_prep_rows_kernel(x_ref, xp_ref, r_ref, k_ref, v_ref, rp_ref, kp_ref, vp_ref, *rest):
    rkv = jnp.concatenate([_cat_blocks(r_ref), _cat_blocks(k_ref), _cat_blocks(v_ref)], axis=1)
    rkv_prev = jnp.concatenate([_cat_blocks(rp_ref), _cat_blocks(kp_ref), _cat_blocks(vp_ref)], axis=1)
    _rwkv_prep_core(x_ref[...], xp_ref[...], rkv, rkv_prev, *rest)


def _rwkv_prep_prompt(x, p, params, n_seq, t, tm):
    nt = t // tm

    def grp(gi):
        return pl.BlockSpec((3, tm, LANES), lambda b, ti: (gi, b * nt + ti, 0))

    def full(a):
        return pl.BlockSpec(a.shape, lambda b, ti: (0,) * a.ndim)

    scan_out = jax.ShapeDtypeStruct((3, t, n_seq * LANES), f32)
    tok_out = jax.ShapeDtypeStruct((3, n_seq * t, LANES), f32)
    return pl.pallas_call(
        _rwkv_prep_seq_kernel,
        grid=(n_seq, nt),
        in_specs=[pl.BlockSpec((tm, D_MODEL), lambda b, ti: (b * nt + ti, 0)), grp(0), grp(1), grp(2)]
        + [full(a) for a in params],
        out_specs=[pl.BlockSpec((3, tm, LANES), lambda b, ti: (0, ti, b))] * 7
        + [pl.BlockSpec((3, tm, LANES), lambda b, ti: (0, b * nt + ti, 0))] * 2,
        out_shape=[scan_out] * 7 + [tok_out] * 2,
        scratch_shapes=[pltpu.VMEM((8, D_MODEL), f32), pltpu.VMEM((8, 3 * GROUP), f32)],
        compiler_params=_cparams(("parallel", "arbitrary")),
        name="rwkv_prep_prompt",
    )(x, p, p, p, *params)


def _rwkv_prep_rows(x, xp, p, pp, params, row0, n_rows, tm):
    i0 = row0 // tm

    def grp(gi):
        return pl.BlockSpec((3, tm, LANES), lambda i: (gi, i0 + i, 0))

    def grp_prev(gi):
        return pl.BlockSpec((3, tm, LANES), lambda i: (gi, i, 0))

    def full(a):
        return pl.BlockSpec(a.shape, lambda i: (0,) * a.ndim)

    out = jax.ShapeDtypeStruct((3, n_rows, LANES), f32)
    return pl.pallas_call(
        _rwkv_prep_rows_kernel,
        grid=(n_rows // tm,),
        in_specs=[pl.BlockSpec((tm, D_MODEL), lambda i: (i0 + i, 0)), pl.BlockSpec((tm, D_MODEL), lambda i: (i, 0)),
                  grp(0), grp(1), grp(2), grp_prev(0), grp_prev(1), grp_prev(2)] + [full(a) for a in params],
        out_specs=[pl.BlockSpec((3, tm, LANES), lambda i: (0, i, 0))] * 9,
        out_shape=[out] * 9,
        compiler_params=_cparams(("parallel",)),
        name="rwkv_prep_rows",
    )(x, xp, p, p, p, pp, pp, pp, *params)


def _rwkv_scan_kernel(w_ref, k_ref, v_ref, kk_ref, b_ref, u_ref, kr_ref, s0_ref, seg_ref, y_ref, so_ref, s_scr,
                      *, n_steps):
    ti = pl.program_id(1)
    bb = w_ref.shape[2] // LANES
    rows = 3 * bb * HEAD_DIM

    @pl.when(ti == 0)
    def _():
        s_scr[...] = s0_ref[...].reshape(rows, LANES)

    if n_steps < y_ref.shape[1]:
        y_ref[...] = jnp.zeros(y_ref.shape, f32)
    eye = ((lax.broadcasted_iota(jnp.int32, (rows, LANES), 0) % HEAD_DIM)
           == (lax.broadcasted_iota(jnp.int32, (rows, LANES), 1) % HEAD_DIM))
    seg = seg_ref[...]

    def rows_of(ref, t):
        tiles = []
        for c in range(3):
            row = ref[c, pl.ds(t, 1), :]
            tiles += [jnp.broadcast_to(row[:, LANES * b:LANES * (b + 1)], (HEAD_DIM, LANES)) for b in range(bb)]
        return jnp.concatenate(tiles, axis=0)

    def step(t, carry):
        s = s_scr[...]
        p = (s * rows_of(kk_ref, t)).astype(bf16)
        q = (s * rows_of(u_ref, t)).astype(bf16)
        vm = jnp.where(eye, rows_of(v_ref, t), 0.0)
        vhi = vm.astype(bf16)
        vlo = (vm - vhi.astype(f32)).astype(bf16)
        red1 = jnp.dot(jnp.concatenate([p, q], axis=0), seg, preferred_element_type=f32)
        red2 = jnp.dot(jnp.concatenate([vhi, vlo], axis=0), seg, preferred_element_type=f32)
        sa = red1[0:rows]
        vcol = red2[0:rows] + red2[rows:2 * rows]
        ycol = red1[rows:2 * rows] + vcol * rows_of(kr_ref, t)
        s_scr[...] = s * rows_of(w_ref, t) - sa * rows_of(b_ref, t) + vcol * rows_of(k_ref, t)
        ym = jnp.where(eye, ycol, 0.0)
        for c in range(3):
            y_ref[c, pl.ds(t, 1), :] = jnp.concatenate(
                [jnp.sum(ym[(c * bb + b) * HEAD_DIM:(c * bb + b + 1) * HEAD_DIM], axis=0, keepdims=True)
                 for b in range(bb)], axis=1)
        return carry

    lax.fori_loop(0, n_steps, step, 0)

    @pl.when(ti == pl.num_programs(1) - 1)
    def _():
        so_ref[...] = s_scr[...].reshape(so_ref.shape)


def _rwkv_scan(ops, s0, seg, bb, tc, n_steps):
    _, t, lanes = ops[0].shape
    b = lanes // LANES
    blk = pl.BlockSpec((3, tc, bb * LANES), lambda bi, ti: (0, ti, bi))
    st = pl.BlockSpec((3, bb, HEAD_DIM, LANES), lambda bi, ti: (0, bi, 0, 0))
    return pl.pallas_call(
        functools.partial(_rwkv_scan_kernel, n_steps=n_steps),
        grid=(b // bb, t // tc),
        in_specs=[blk] * 7 + [st, pl.BlockSpec((LANES, LANES), lambda bi, ti: (0, 0))],
        out_specs=[blk, st],
        out_shape=[jax.ShapeDtypeStruct((3, t, b * LANES), f32),
                   jax.ShapeDtypeStruct((3, b, HEAD_DIM, LANES), f32)],
        scratch_shapes=[pltpu.VMEM((3 * bb * HEAD_DIM, LANES), f32)],
        compiler_params=_cparams(("parallel", "arbitrary")),
        name="rwkv_scan",
    )(*ops, s0, seg)


def _log_gamma(h):
    return float(np.log(np.float32(1.0) - np.float32(2.0) ** np.float32(-5.0 - min(h, H_RET - 1))))


def _retention_kernel(q_ref, k_ref, v_ref, g_ref, s0_ref, seg_ref, y_ref, so_ref, s_scr, *, n_valid):
    ci = pl.program_id(1)
    rows = q_ref.shape[1]

    @pl.when(ci == 0)
    def _():
        s_scr[...] = s0_ref[0]

    lo = _lane_lo((rows, LANES))
    ii = lax.broadcasted_iota(jnp.int32, (rows, rows), 0)
    jj = lax.broadcasted_iota(jnp.int32, (rows, rows), 1)
    rel = (ii - jj).astype(f32)
    causal = ii >= jj
    pos = lax.broadcasted_iota(jnp.int32, (rows, LANES), 0)
    posf = pos.astype(f32)
    row_ok = pos < n_valid
    sq_r = lax.broadcasted_iota(jnp.int32, (LANES, LANES), 0) < HEAD_DIM
    sq_c = lax.broadcasted_iota(jnp.int32, (LANES, LANES), 1) < HEAD_DIM
    same_head = sq_r == sq_c
    seg = seg_ref[...]

    for c in range(3):
        lg0, lg1 = _log_gamma(2 * c), _log_gamma(2 * c + 1)
        lg_lane = jnp.where(lo, lg0, lg1)
        q = q_ref[c]
        q16 = q.astype(bf16)
        k = jnp.where(row_ok, k_ref[c], 0.0)
        k16 = k.astype(bf16)
        v16 = v_ref[c].astype(bf16)
        o_halves = []
        for half, lg in ((0, lg0), (1, lg1)):
            qm = jnp.where(lo if half == 0 else jnp.logical_not(lo), q, 0.0).astype(bf16)
            s = lax.dot_general(qm, k16, _NT, preferred_element_type=f32)
            dmat = jnp.where(causal, jnp.exp(lg * jnp.maximum(rel, 0.0)), 0.0)
            o_halves.append(jnp.dot((s * dmat).astype(bf16), v16, preferred_element_type=f32))
        o = jnp.where(lo, o_halves[0], o_halves[1])
        state = s_scr[c]
        cross = jnp.dot(q16, state.astype(bf16), preferred_element_type=f32)
        o = o + cross * jnp.exp(lg_lane * (posf + 1.0))
        k_in = (k * jnp.exp(lg_lane * (float(n_valid) - 1.0 - posf))).astype(bf16)
        upd = lax.dot_general(k_in, v16, _TN, preferred_element_type=f32)
        lg_col = jnp.where(sq_c, lg0, lg1)
        s_scr[c] = state * jnp.exp(lg_col * float(n_valid)) + jnp.where(same_head, upd, 0.0)

        mean = jnp.dot(o, seg, precision=HIGHEST, preferred_element_type=f32)
        d = o - mean
        var = jnp.dot(d * d, seg, precision=HIGHEST, preferred_element_type=f32)
        yn = d * lax.rsqrt(var + RET_GN_EPS)
        g = g_ref[c]
        y_ref[c] = g * jax.nn.sigmoid(g) * yn

    @pl.when(ci == pl.num_programs(1) - 1)
    def _():
        so_ref[0] = s_scr[...]


def _retention(p, s0_pairs, seg, n_seq, rows, n_chunks, n_valid, row0):
    def grp(gi):
        return pl.BlockSpec((3, rows, LANES), lambda b, c: (gi, row0 // rows + b * n_chunks + c, 0))

    st = pl.BlockSpec((1, 3, LANES, LANES), lambda b, c: (b, 0, 0, 0))
    return pl.pallas_call(
        functools.partial(_retention_kernel, n_valid=n_valid),
        grid=(n_seq, n_chunks),
        in_specs=[grp(3), grp(4), grp(5), grp(6), st, pl.BlockSpec(seg.shape, lambda b, c: (0, 0))],
        out_specs=[pl.BlockSpec((3, rows, LANES), lambda b, c: (0, b * n_chunks + c, 0)), st],
        out_shape=[jax.ShapeDtypeStruct((3, n_seq * n_chunks * rows, LANES), f32),
                   jax.ShapeDtypeStruct((n_seq, 3, LANES, LANES), f32)],
        scratch_shapes=[pltpu.VMEM((3, LANES, LANES), f32)],
        compiler_params=_cparams(("parallel", "arbitrary")),
        name="retention",
    )(p, p, p, p, s0_pairs, seg)


def _dil_prompt_kernel(q_ref, k_ref, v_ref, y_ref, o_scr, l_scr):
    i = pl.program_id(1)
    n_steps = pl.num_programs(1)
    blk = DIL_BLK
    lo = _lane_lo((blk, LANES))
    ii = lax.broadcasted_iota(jnp.int32, (blk, 2 * blk), 0)
    jj = lax.broadcasted_iota(jnp.int32, (blk, 2 * blk), 1)
    band = jnp.logical_and(jj >= ii, jj <= ii + blk)

    for p, dil in enumerate(DILATIONS):
        nb = q_ref.shape[1] // (dil * blk)
        r = i // nb
        cb = i % nb
        start_q = r + dil * blk * cb
        start_p = r + dil * blk * jnp.maximum(cb - 1, 0)
        valid = jnp.logical_and(band, jnp.logical_or(cb > 0, jj >= blk))

        def rows(ref, c, start):
            if dil == 1:
                return ref[c, pl.ds(start, blk), :]
            return ref[c, pl.ds(start, blk, stride=dil), :]

        for c in range(3):
            q = rows(q_ref, c, start_q)
            kwin = jnp.concatenate([rows(k_ref, c, start_p), rows(k_ref, c, start_q)], axis=0).astype(bf16)
            vwin = jnp.concatenate([rows(v_ref, c, start_p), rows(v_ref, c, start_q)], axis=0).astype(bf16)
            outs, lses = [], []
            for half in range(2):
                qm = jnp.where(lo if half == 0 else jnp.logical_not(lo), q, 0.0).astype(bf16)
                s = lax.dot_general(qm, kwin, _NT, preferred_element_type=f32)
                s = jnp.where(valid, s, -jnp.inf)
                mx = jnp.max(s, axis=-1, keepdims=True)
                lse = mx + jnp.log(jnp.sum(jnp.exp(s - mx), axis=-1, keepdims=True))
                prob = jnp.exp(s - lse).astype(bf16)
                outs.append(jnp.dot(prob, vwin, preferred_element_type=f32))
                lses.append(lse)
            o_pair = jnp.where(lo, outs[0], outs[1])
            l_pair = jnp.where(lo, lses[0], lses[1])
            if dil == 1:
                o_scr[p, c, pl.ds(start_q, blk), :] = o_pair
                l_scr[p, c, pl.ds(start_q, blk), :] = l_pair
            else:
                o_scr[p, c, pl.ds(start_q, blk, stride=dil), :] = o_pair
                l_scr[p, c, pl.ds(start_q, blk, stride=dil), :] = l_pair

    @pl.when(i == n_steps - 1)
    def _():
        def merge(j, carry):
            sl = pl.ds(pl.multiple_of(j * blk, blk), blk)
            for c in range(3):
                ls = [l_scr[p, c, sl, :] for p in range(3)]
                mx = jnp.maximum(jnp.maximum(ls[0], ls[1]), ls[2])
                es = [jnp.exp(l - mx) for l in ls]
                tot = es[0] + es[1] + es[2]
                acc = (es[0] / tot) * o_scr[0, c, sl, :]
                acc = acc + (es[1] / tot) * o_scr[1, c, sl, :]
                acc = acc + (es[2] / tot) * o_scr[2, c, sl, :]
                y_ref[c, sl, :] = acc
            return carry

        lax.fori_loop(0, q_ref.shape[1] // blk, merge, 0)


def _dil_prompt(p, n_seq, t):
    def grp(gi):
        return pl.BlockSpec((3, t, LANES), lambda b, i: (gi, b, 0))

    n_steps = t // DIL_BLK
    return pl.pallas_call(
        _dil_prompt_kernel,
        grid=(n_seq, n_steps),
        in_specs=[grp(7), grp(8), grp(9)],
        out_specs=pl.BlockSpec((3, t, LANES), lambda b, i: (0, b, 0)),
        out_shape=jax.ShapeDtypeStruct((3, n_seq * t, LANES), f32),
        scratch_shapes=[pltpu.VMEM((3, 3, t, LANES), f32), pltpu.VMEM((3, 3, t, LANES), f32)],
        compiler_params=_cparams(("parallel", "arbitrary")),
        name="dil_prompt",
    )(p, p, p)


N_NEW = 4


def _dil_sample_kernel(q_ref, kn_ref, vn_ref, kc_ref, vc_ref, *rest, n_buf, aliased):
    y_ref, ko_ref, vo_ref = rest[2:] if aliased else rest
    t_all = n_buf + LANES
    zpad = jnp.zeros((LANES - DEC_PAD, LANES), f32)
    qi = lax.broadcasted_iota(jnp.int32, (DEC_PAD, t_all), 0)
    tok = lax.broadcasted_iota(jnp.int32, (DEC_PAD, t_all), 1)
    dist = n_buf + qi - tok
    masks = [jnp.logical_and(jnp.logical_and(dist >= 0, dist <= DIL_BLK * dil), (dist & (dil - 1)) == 0)
             for dil in DILATIONS]
    row_ok = lax.broadcasted_iota(jnp.int32, (DEC_PAD, LANES), 0) < N_NEW
    tail_ok = lax.broadcasted_iota(jnp.int32, (HEAD_DIM, LANES), 1) < N_NEW

    for c in range(3):
        kn_t = jnp.concatenate([kn_ref[c], zpad], axis=0).T
        vn_t = jnp.concatenate([vn_ref[c], zpad], axis=0).T
        q_pair = q_ref[c]
        y_halves = []
        for half in range(2):
            h = 2 * c + half
            k_tail = jnp.where(tail_ok, kn_t[HEAD_DIM * half:HEAD_DIM * (half + 1)], 0.0)
            v_tail = jnp.where(tail_ok, vn_t[HEAD_DIM * half:HEAD_DIM * (half + 1)], 0.0)
            k_all = jnp.concatenate([kc_ref[0, 0, h], k_tail], axis=1)
            v_all = jnp.concatenate([vc_ref[0, 0, h], v_tail], axis=1)
            ko_ref[0, 0, h] = pltpu.roll(k_all, t_all - N_NEW, 1)[:, :n_buf]
            vo_ref[0, 0, h] = pltpu.roll(v_all, t_all - N_NEW, 1)[:, :n_buf]
            q16 = q_pair[:, HEAD_DIM * half:HEAD_DIM * (half + 1)].astype(bf16)
            s = jnp.dot(q16, k_all.astype(bf16), preferred_element_type=f32)
            probs, lses = [], []
            for msk in masks:
                sm = jnp.where(msk, s, -jnp.inf)
                mx = jnp.max(sm, axis=-1, keepdims=True)
                lse = mx + jnp.log(jnp.sum(jnp.exp(sm - mx), axis=-1, keepdims=True))
                probs.append(jnp.exp(sm - lse).astype(bf16))
                lses.append(lse)
            o = lax.dot_general(jnp.concatenate(probs, axis=0), v_all.astype(bf16), _NT,
                                preferred_element_type=f32)
            mx = jnp.maximum(jnp.maximum(lses[0], lses[1]), lses[2])
            es = [jnp.exp(l - mx) for l in lses]
            tot = es[0] + es[1] + es[2]
            y_halves.append((es[0] / tot) * o[0:DEC_PAD] + (es[1] / tot) * o[DEC_PAD:2 * DEC_PAD]
                            + (es[2] / tot) * o[2 * DEC_PAD:3 * DEC_PAD])
        y_ref[c] = jnp.where(row_ok, jnp.concatenate(y_halves, axis=1), 0.0)


def _dil_sample(p, cache_k_t, cache_v_t, layer, prev_outs, n_seq, row0):
    n_buf = cache_k_t.shape[-1]

    def grp(gi):
        return pl.BlockSpec((3, DEC_PAD, LANES), lambda b: (gi, row0 // DEC_PAD + b, 0))

    buf = pl.BlockSpec((1, 1, H_DIL, HEAD_DIM, n_buf), lambda b: (layer, b, 0, 0, 0))
    aliased = prev_outs is not None
    extra = list(prev_outs) if aliased else []
    return pl.pallas_call(
        functools.partial(_dil_sample_kernel, n_buf=n_buf, aliased=aliased),
        grid=(n_seq,),
        in_specs=[grp(7), grp(8), grp(9), buf, buf] + [pl.BlockSpec(memory_space=pl.ANY)] * len(extra),
        out_specs=[pl.BlockSpec((3, DEC_PAD, LANES), lambda b: (0, b, 0)), buf, buf],
        out_shape=[jax.ShapeDtypeStruct((3, n_seq * DEC_PAD, LANES), f32),
                   jax.ShapeDtypeStruct(cache_k_t.shape, f32),
                   jax.ShapeDtypeStruct(cache_v_t.shape, f32)],
        input_output_aliases={5: 1, 6: 2} if aliased else {},
        compiler_params=_cparams(("arbitrary",)),
        name="dil_sample",
    )(p, p, p, cache_k_t, cache_v_t, *extra)


def _layer_norm(z, gamma, beta):
    mu = jnp.mean(z, axis=-1, keepdims=True)
    d = z - mu
    var = jnp.mean(d * d, axis=-1, keepdims=True)
    return d * lax.rsqrt(var + LN_EPS) * gamma + beta


def _merge_kernel(ya_ref, bon_ref, g_ref, yb_ref, yc_ref, x_ref, lnx_ref, seg_ref, wo_ref, ln_ref, *rest):
    o_ref = rest[-1]
    yr = _cat_blocks(ya_ref)
    seg = seg_ref[...]
    mean = jnp.dot(yr, seg, precision=HIGHEST, preferred_element_type=f32)
    d = yr - mean
    var = jnp.dot(d * d, seg, precision=HIGHEST, preferred_element_type=f32)
    ya = d * lax.rsqrt(var + RWKV_GN_EPS) * lnx_ref[0:1, :] + lnx_ref[1:2, :]
    ya = (ya + _cat_blocks(bon_ref)) * _cat_blocks(g_ref)
    ycat = jnp.concatenate([ya, _cat_blocks(yb_ref), _cat_blocks(yc_ref)], axis=1).astype(bf16)
    h = jnp.dot(ycat, wo_ref[...], preferred_element_type=f32)
    o_ref[...] = _layer_norm(DEEPNORM_ALPHA * x_ref[...] + h, ln_ref[0:1, :], ln_ref[1:2, :])


def _merge_prompt(ya_tm, bonus, g, yb, yc, x, params, n_seq, t, tm):
    nt = t // tm
    tok = pl.BlockSpec((3, tm, LANES), lambda b, ti: (0, b * nt + ti, 0))
    rows = pl.BlockSpec((tm, D_MODEL), lambda b, ti: (b * nt + ti, 0))

    def full(a):
        return pl.BlockSpec(a.shape, lambda b, ti: (0,) * a.ndim)

    return pl.pallas_call(
        _merge_kernel,
        grid=(n_seq, nt),
        in_specs=[pl.BlockSpec((3, tm, LANES), lambda b, ti: (0, ti, b)), tok, tok, tok, tok, rows]
        + [full(a) for a in params],
        out_specs=rows,
        out_shape=jax.ShapeDtypeStruct(x.shape, f32),
        compiler_params=_cparams(("parallel", "parallel")),
        name="mixer_merge_prompt",
    )(ya_tm, bonus, g, yb, yc, x, *params)


def _merge_rows(ya, bonus, g, yb, yc, x, params, x_out, row0, tm):
    i0 = row0 // tm
    blk = pl.BlockSpec((3, tm, LANES), lambda i: (0, i, 0))
    rows = pl.BlockSpec((tm, D_MODEL), lambda i: (i0 + i, 0))

    def full(a):
        return pl.BlockSpec(a.shape, lambda i: (0,) * a.ndim)

    return pl.pallas_call(
        _merge_kernel,
        grid=((x.shape[0] - row0) // tm,),
        in_specs=[blk] * 5 + [rows] + [full(a) for a in params] + [pl.BlockSpec(memory_space=pl.ANY)],
        out_specs=rows,
        out_shape=jax.ShapeDtypeStruct(x.shape, f32),
        input_output_aliases={6 + len(params): 0},
        compiler_params=_cparams(("parallel",)),
        name="mixer_merge_rows",
    )(ya, bonus, g, yb, yc, x, *params, x_out)


EXPERT_BLOCK = 8


def _first_max(x, idx, axes, sentinel):
    mx = x
    for ax in axes:
        mx = jnp.max(mx, axis=ax, keepdims=True)
    am = jnp.where(x == mx, idx, sentinel)
    for ax in axes:
        am = jnp.min(am, axis=ax, keepdims=True)
    return mx, am


def _route(logits_t, bias_col):
    n_tok = logits_t.shape[1]
    per = N_EXPERTS // N_ROUTE_GROUPS
    scores = jax.nn.sigmoid(logits_t)
    shp = (N_ROUTE_GROUPS, per, n_tok)
    s3 = scores.reshape(shp)
    b3 = (scores + bias_col).reshape(shp)
    sub = lax.broadcasted_iota(jnp.int32, shp, 1)
    grp = lax.broadcasted_iota(jnp.int32, shp, 0)
    eid = grp * per + sub
    neg = -jnp.inf

    m1, i1 = _first_max(b3, sub, (1,), per)
    m2, _ = _first_max(jnp.where(sub == i1, neg, b3), sub, (1,), per)
    gscore = m1 + m2
    gid = lax.broadcasted_iota(jnp.int32, gscore.shape, 0)
    gsel = jnp.zeros(gscore.shape, jnp.bool_)
    for _ in range(TOPK_GROUPS):
        _, gi = _first_max(gscore, gid, (0,), N_ROUTE_GROUPS)
        hit = gid == gi
        gsel = jnp.logical_or(gsel, hit)
        gscore = jnp.where(hit, neg, gscore)
    masked = jnp.where(gsel, b3, neg)
    esel = jnp.zeros(shp, jnp.bool_)
    for _ in range(TOP_K):
        _, ei = _first_max(masked, eid, (1, 0), N_EXPERTS)
        hit = eid == ei
        esel = jnp.logical_or(esel, hit)
        masked = jnp.where(hit, neg, masked)
    top_s = jnp.where(esel, s3, 0.0)
    denom = jnp.sum(jnp.sum(top_s, axis=1, keepdims=True), axis=0, keepdims=True)
    return (top_s / denom * ROUTE_SCALE).reshape(N_EXPERTS, n_tok)


def _moe_kernel(x_ref, wr_ref, rb_ref, wg_ref, wu_ref, wd_ref, ex_ref, wsg_ref, wsu_ref, wsd_ref, ln_ref,
                o_ref, x16_scr, chi_scr, clo_scr, acc_scr):
    j = pl.program_id(1)

    @pl.when(j == 0)
    def _():
        x16 = x_ref[...].astype(bf16)
        x16_scr[...] = x16
        logits_t = lax.dot_general(wr_ref[...], x16, _NT, preferred_element_type=f32)
        comb_t = _route(logits_t, rb_ref[...])
        comb_t = jnp.concatenate([comb_t, jnp.zeros_like(comb_t)], axis=0)
        comb = comb_t.T
        hi = comb.astype(bf16)
        chi_scr[...] = hi
        clo_scr[...] = (comb - hi.astype(f32)).astype(bf16)
        sg = jnp.dot(x16, wsg_ref[...], preferred_element_type=f32)
        su = jnp.dot(x16, wsu_ref[...], preferred_element_type=f32)
        hs = (sg * jax.nn.sigmoid(sg) * su).astype(bf16)
        acc_scr[...] = jnp.dot(hs, wsd_ref[...], preferred_element_type=f32)

    x16 = x16_scr[...]
    gate = jnp.dot(x16, wg_ref[...], preferred_element_type=f32)
    up = jnp.dot(x16, wu_ref[...], preferred_element_type=f32)
    ex = ex_ref[...]
    cexp = (jnp.dot(chi_scr[...], ex, preferred_element_type=f32)
            + jnp.dot(clo_scr[...], ex, preferred_element_type=f32))
    h = (gate * jax.nn.sigmoid(gate) * up * cexp).astype(bf16)
    acc_scr[...] += jnp.dot(h, wd_ref[...], preferred_element_type=f32)

    @pl.when(j == pl.num_programs(1) - 1)
    def _():
        o_ref[...] = _layer_norm(DEEPNORM_ALPHA * x_ref[...] + acc_scr[...], ln_ref[0:1, :], ln_ref[1:2, :])


def _moe(x, wr_t, rb_col, wg, wu, wd, expand, wsg, wsu, wsd, ln, tm):
    m = x.shape[0]
    nb = EXPERT_BLOCK * D_EXPERT

    def full(a):
        return pl.BlockSpec(a.shape, lambda i, j: (0,) * a.ndim)

    return pl.pallas_call(
        _moe_kernel,
        grid=(m // tm, N_EXPERTS // EXPERT_BLOCK),
        in_specs=[pl.BlockSpec((tm, D_MODEL), lambda i, j: (i, 0)), full(wr_t), full(rb_col),
                  pl.BlockSpec((D_MODEL, nb), lambda i, j: (0, j)),
                  pl.BlockSpec((D_MODEL, nb), lambda i, j: (0, j)),
                  pl.BlockSpec((nb, D_MODEL), lambda i, j: (j, 0)),
                  pl.BlockSpec((LANES, nb), lambda i, j: (0, j)),
                  full(wsg), full(wsu), full(wsd), full(ln)],
        out_specs=pl.BlockSpec((tm, D_MODEL), lambda i, j: (i, 0)),
        out_shape=jax.ShapeDtypeStruct((m, D_MODEL), f32),
        scratch_shapes=[pltpu.VMEM((tm, D_MODEL), bf16), pltpu.VMEM((tm, LANES), bf16),
                        pltpu.VMEM((tm, LANES), bf16), pltpu.VMEM((tm, D_MODEL), f32)],
        compiler_params=_cparams(("parallel", "arbitrary")),
        name="moe",
    )(x, wr_t, rb_col, wg, wu, wd, expand, wsg, wsu, wsd, ln)


def _pad_heads(a, n_heads):
    return jnp.pad(a, [(0, 0)] * (a.ndim - 1) + [(0, GROUP - n_heads * HEAD_DIM)])


def _pad_w_in(w):
    widths = [C_RWKV] * 3 + [C_RET] * 4 + [C_DIL] * 3
    parts, off = [], 0
    for wd in widths:
        parts.append(jnp.pad(w[:, off:off + wd], ((0, 0), (0, GROUP - wd))))
        off += wd
    return jnp.concatenate(parts, axis=1).astype(bf16)


def _rope_tables(pos, inv_freq):
    ang = pos.astype(f32)[:, None] * inv_freq[None, :]
    cos, sin = jnp.cos(ang), jnp.sin(ang)
    return jnp.tile(jnp.concatenate([cos, cos], axis=1), (1, 2)), jnp.tile(jnp.concatenate([-sin, sin], axis=1), (1, 2))


def _segment_matrix(width, scale):
    idx = np.arange(width) // HEAD_DIM
    return jnp.asarray((idx[:, None] == idx[None, :]).astype(np.float32) * scale)


def _pairs_from_heads(s):
    b = s.shape[0]
    s6 = jnp.pad(s, ((0, 0), (0, 1), (0, 0), (0, 0))).reshape(b, 3, 2, HEAD_DIM, HEAD_DIM)
    z = jnp.zeros((b, 3, HEAD_DIM, HEAD_DIM), s.dtype)
    top = jnp.concatenate([s6[:, :, 0], z], axis=-1)
    bot = jnp.concatenate([z, s6[:, :, 1]], axis=-1)
    return jnp.concatenate([top, bot], axis=-2)


def _heads_from_pairs(sp):
    b = sp.shape[0]
    h0 = sp[:, :, :HEAD_DIM, :HEAD_DIM]
    h1 = sp[:, :, HEAD_DIM:, HEAD_DIM:]
    return jnp.stack([h0, h1], axis=2).reshape(b, 6, HEAD_DIM, HEAD_DIM)[:, :H_RET]


def _rwkv_pairs_from_heads(s):
    b = s.shape[0]
    s6 = jnp.pad(s, ((0, 0), (0, 1), (0, 0), (0, 0))).reshape(b, 3, 2, HEAD_DIM, HEAD_DIM)
    return jnp.transpose(s6, (1, 0, 3, 2, 4)).reshape(3, b, HEAD_DIM, LANES)


def _rwkv_heads_from_pairs(sp):
    b = sp.shape[1]
    s6 = jnp.transpose(sp.reshape(3, b, HEAD_DIM, 2, HEAD_DIM), (1, 0, 3, 2, 4))
    return s6.reshape(b, 6, HEAD_DIM, HEAD_DIM)[:, :H_RWKV]


def kernel(x_prompt, x_sample, state_rwkv, state_shift, state_ret, cache_k_win, cache_v_win, w_in, mu_rkv, mu_wag, w0, w1, w2, a0, a1, a2, g1, g2, k_k, k_a, r_k, lnx_g, lnx_b, w_out, ln1_g, ln1_b, w_router, router_bias, w_gate, w_up, w_down, ws_gate, ws_up, ws_down, ln2_g, ln2_b):
    bp, tp, _ = x_prompt.shape
    bs, ts, _ = x_sample.shape
    n_buf = cache_k_win.shape[2]
    depth = w_in.shape[0]
    assert ts == N_NEW and n_buf == DILATIONS[-1] * DIL_BLK and tp % (DILATIONS[-1] * DIL_BLK) == 0
    mp = bp * tp
    ms = bs * DEC_PAD
    m = mp + ms
    tm = 256
    tm_moe = 512
    assert m % tm_moe == 0 and bs % 8 == 0

    pos = jnp.concatenate([jnp.tile(jnp.arange(tp), bp), jnp.tile(PAST_LEN + jnp.arange(DEC_PAD), bs)])
    inv_ret = 1.0 / (10000.0 ** jnp.linspace(0.0, 1.0, HEAD_DIM // 2, dtype=f32))
    inv_dil = ROPE_THETA ** (-jnp.arange(0, HEAD_DIM, 2, dtype=f32) / HEAD_DIM)
    cr, sr = _rope_tables(pos, inv_ret)
    cd, sd = _rope_tables(pos, inv_dil)
    zero_tab = jnp.zeros((bs, LANES), f32)

    seg_sum = _segment_matrix(GROUP, 1.0)
    seg_mean = _segment_matrix(GROUP, 1.0 / HEAD_DIM)
    seg_mean_pair = _segment_matrix(LANES, 1.0 / HEAD_DIM)
    seg_pair_bf16 = _segment_matrix(LANES, 1.0).astype(bf16)
    expand = jnp.asarray((np.arange(LANES)[:, None] == (np.arange(N_EXPERTS * D_EXPERT) // D_EXPERT)[None, :])
                         .astype(np.float32)).astype(bf16)

    xs_pad = jnp.pad(x_sample, ((0, 0), (0, DEC_PAD - ts), (0, 0)))
    x = jnp.concatenate([x_prompt.reshape(mp, D_MODEL), xs_pad.reshape(ms, D_MODEL)], axis=0)

    cache_k_t = jnp.transpose(cache_k_win, (0, 1, 3, 4, 2))
    cache_v_t = jnp.transpose(cache_v_win, (0, 1, 3, 4, 2))
    win_outs = None

    outs_p = [[] for _ in range(5)]
    outs_s = [[] for _ in range(3)]
    for l in range(depth):
        xp3 = x[:mp].reshape(bp, tp, D_MODEL)
        xs3 = x[mp:].reshape(bs, DEC_PAD, D_MODEL)
        x_last_s = state_shift[l]
        outs_p[1].append(xp3[:, -1])
        outs_s[1].append(xs3[:, ts - 1])

        w_in_p = _pad_w_in(w_in[l])
        p, k_new, v_new = _inproj(x, w_in_p, cr, sr, cd, sd, tm)
        p_last, _, _ = _inproj(x_last_s, w_in_p, zero_tab, zero_tab, zero_tab, zero_tab, min(bs, 128))

        xs_prev = jnp.concatenate([x_last_s[:, None], xs3[:, :-1]], axis=1).reshape(ms, D_MODEL)
        pr_s = p[:9, mp:].reshape(9, bs, DEC_PAD, LANES)
        pp_s = jnp.concatenate([p_last[:9][:, :, None], pr_s[:, :, :-1]], axis=2).reshape(9, ms, LANES)

        vec = jnp.stack([_pad_heads(v, H_RWKV) for v in (w0[l], a0[l], k_k[l], k_a[l], r_k[l].reshape(-1))])
        prep_params = (_pad_heads(mu_rkv[l], H_RWKV), mu_wag[l], vec,
                       w1[l].astype(bf16), _pad_heads(w2[l], H_RWKV).astype(bf16),
                       a1[l].astype(bf16), _pad_heads(a2[l], H_RWKV).astype(bf16),
                       g1[l].astype(bf16), _pad_heads(g2[l], H_RWKV).astype(bf16), seg_sum)
        prep_p = _rwkv_prep_prompt(x, p, prep_params, bp, tp, tm)
        prep_s = _rwkv_prep_rows(x, xs_prev, p, pp_s, prep_params, mp, ms, tm)

        ops_s = [jnp.swapaxes(a.reshape(3, bs, DEC_PAD, LANES), 1, 2).reshape(3, DEC_PAD, bs * LANES)
                 for a in prep_s[:7]]
        ya_p, s_rwkv_p = _rwkv_scan(list(prep_p[:7]), jnp.zeros((3, bp, HEAD_DIM, LANES), f32), seg_pair_bf16,
                                    bp, 64, 64)
        ya_s, s_rwkv_s = _rwkv_scan(ops_s, _rwkv_pairs_from_heads(state_rwkv[l]), seg_pair_bf16, 8, DEC_PAD, ts)
        ya_s = jnp.swapaxes(ya_s.reshape(3, DEC_PAD, bs, LANES), 1, 2).reshape(3, ms, LANES)
        outs_p[0].append(_rwkv_heads_from_pairs(s_rwkv_p))
        outs_s[0].append(_rwkv_heads_from_pairs(s_rwkv_s))

        yb_p, s_ret_p = _retention(p, jnp.zeros((bp, 3, LANES, LANES), f32), seg_mean_pair,
                                   bp, RET_CHUNK, tp // RET_CHUNK, RET_CHUNK, 0)
        yb_s, s_ret_s = _retention(p, _pairs_from_heads(state_ret[l]), seg_mean_pair, bs, DEC_PAD, 1, ts, mp)
        outs_p[2].append(_heads_from_pairs(s_ret_p))
        outs_s[2].append(_heads_from_pairs(s_ret_s))

        yc_p = _dil_prompt(p, bp, tp)
        yc_s, k_win_t, v_win_t = _dil_sample(p, cache_k_t, cache_v_t, l, win_outs, bs, mp)
        win_outs = (k_win_t, v_win_t)
        keep = min(n_buf, tp)
        outs_p[3].append(k_new[:mp].reshape(bp, tp, H_DIL, HEAD_DIM)[:, tp - keep:])
        outs_p[4].append(v_new[:mp].reshape(bp, tp, H_DIL, HEAD_DIM)[:, tp - keep:])

        w_out_p = jnp.concatenate([
            jnp.pad(w_out[l][:C_RWKV], ((0, GROUP - C_RWKV), (0, 0))),
            jnp.pad(w_out[l][C_RWKV:C_RWKV + C_RET], ((0, GROUP - C_RET), (0, 0))),
            w_out[l][C_RWKV + C_RET:]], axis=0).astype(bf16)
        lnx = jnp.stack([_pad_heads(lnx_g[l], H_RWKV), _pad_heads(lnx_b[l], H_RWKV)])
        merge_params = (lnx, seg_mean, w_out_p, jnp.stack([ln1_g[l], ln1_b[l]]))
        x_new = _merge_prompt(ya_p, prep_p[8], prep_p[7], yb_p, yc_p, x, merge_params, bp, tp, tm)
        x = _merge_rows(ya_s, prep_s[8], prep_s[7], yb_s, yc_s, x, merge_params, x_new, mp, tm)

        wg = jnp.transpose(w_gate[l], (1, 0, 2)).reshape(D_MODEL, N_EXPERTS * D_EXPERT).astype(bf16)
        wu = jnp.transpose(w_up[l], (1, 0, 2)).reshape(D_MODEL, N_EXPERTS * D_EXPERT).astype(bf16)
        wd = w_down[l].reshape(N_EXPERTS * D_EXPERT, D_MODEL).astype(bf16)
        x = _moe(x, w_router[l].T.astype(bf16), router_bias[l].reshape(N_EXPERTS, 1), wg, wu, wd, expand,
                 ws_gate[l].astype(bf16), ws_up[l].astype(bf16), ws_down[l].astype(bf16),
                 jnp.stack([ln2_g[l], ln2_b[l]]), tm_moe)

    y_prompt = x[:mp].reshape(bp, tp, D_MODEL)
    y_sample = x[mp:].reshape(bs, DEC_PAD, D_MODEL)[:, :ts]
    wins = tuple(jnp.transpose(w, (0, 1, 4, 2, 3)) for w in win_outs)
    return ((y_prompt, y_sample) + tuple(jnp.stack(o) for o in outs_p)
            + tuple(jnp.stack(o) for o in outs_s) + wins)
```

```python
import functools
import math

import numpy as np
import jax
import jax.numpy as jnp
from jax import lax
from jax.experimental import pallas as pl
from jax.experimental.pallas import tpu as pltpu

f32 = jnp.float32
bf16 = jnp.bfloat16

D_MODEL = 1024
HEAD_DIM = 64
LANES = 128
H_RWKV = 5
H_RET = 5
H_DIL = 6
GROUP = 384
N_GROUPS_IN = 10
N_BLOCKS_IN = 3 * N_GROUPS_IN
C_RWKV = H_RWKV * HEAD_DIM
C_RET = H_RET * HEAD_DIM
C_DIL = H_DIL * HEAD_DIM
PAST_LEN = 2048
DEC_PAD = 8
RET_CHUNK = 128
RWKV_GN_EPS = 64e-5
RET_GN_EPS = 1e-5
DILATIONS = (1, 4, 16)
DIL_BLK = 128
ROPE_THETA = 10000.0
N_EXPERTS = 64
D_EXPERT = 128
TOP_K = 8
N_ROUTE_GROUPS = 8
TOPK_GROUPS = 4
ROUTE_SCALE = 2.5
LN_EPS = 1e-5
DEPTH = 2
DEEPNORM_ALPHA = (2 * DEPTH) ** 0.25
VMEM_LIMIT = 56 * 1024 * 1024
HIGHEST = lax.Precision.HIGHEST

_NT = (((1,), (1,)), ((), ()))
_TN = (((0,), (0,)), ((), ()))


def _cparams(sem):
    return pltpu.CompilerParams(dimension_semantics=sem, vmem_limit_bytes=VMEM_LIMIT)


def _lane_lo(shape):
    return (lax.broadcasted_iota(jnp.int32, shape, len(shape) - 1) % LANES) < HEAD_DIM


def _cat_blocks(ref):
    return jnp.concatenate([ref[0], ref[1], ref[2]], axis=1)


def _put_blocks(ref, val):
    for c in range(3):
        ref[c] = val[:, LANES * c:LANES * (c + 1)]


def _inproj_kernel(x_ref, w_ref, cr_ref, sr_ref, cd_ref, sd_ref, p_ref, kn_ref, vn_ref):
    acc = jnp.dot(x_ref[...].astype(bf16), w_ref[...], preferred_element_type=f32)
    tm = acc.shape[0]
    first = (lax.broadcasted_iota(jnp.int32, (tm, LANES), 1) % HEAD_DIM) < HEAD_DIM // 2

    def rot(xb, c, s):
        partner = jnp.where(first, pltpu.roll(xb, LANES - HEAD_DIM // 2, 1), pltpu.roll(xb, HEAD_DIM // 2, 1))
        return xb * c + partner * s

    for j in range(N_BLOCKS_IN):
        blk = acc[:, LANES * j:LANES * (j + 1)]
        g = j // 3
        if g in (3, 4):
            blk = rot(blk, cr_ref[...], sr_ref[...])
        if g in (7, 8):
            blk = rot(blk, cd_ref[...], sd_ref[...])
        if g in (4, 7):
            blk = blk * (HEAD_DIM ** -0.5)
        p_ref[j] = blk
        if g == 8:
            kn_ref[:, LANES * (j - 24):LANES * (j - 23)] = blk
        if g == 9:
            vn_ref[:, LANES * (j - 27):LANES * (j - 26)] = blk


def _inproj(x, w, cr, sr, cd, sd, tm):
    m = x.shape[0]
    tab = pl.BlockSpec((tm, LANES), lambda i: (i, 0))
    return pl.pallas_call(
        _inproj_kernel,
        grid=(m // tm,),
        in_specs=[pl.BlockSpec((tm, D_MODEL), lambda i: (i, 0)),
                  pl.BlockSpec((D_MODEL, N_BLOCKS_IN * LANES), lambda i: (0, 0)),
                  tab, tab, tab, tab],
        out_specs=[pl.BlockSpec((N_BLOCKS_IN, tm, LANES), lambda i: (0, i, 0)),
                   pl.BlockSpec((tm, GROUP), lambda i: (i, 0)),
                   pl.BlockSpec((tm, GROUP), lambda i: (i, 0))],
        out_shape=[jax.ShapeDtypeStruct((N_BLOCKS_IN, m, LANES), f32),
                   jax.ShapeDtypeStruct((m, GROUP), f32),
                   jax.ShapeDtypeStruct((m, GROUP), f32)],
        compiler_params=_cparams(("parallel",)),
        name="inproj",
    )(x, w, cr, sr, cd, sd)


def _softplus(z):
    return jnp.maximum(z, 0.0) + jnp.log(1.0 + jnp.exp(-jnp.abs(z)))


def _rwkv_prep_core(x, x_prev, rkv, rkv_prev,
                    mu_rkv_ref, mu_wag_ref, vec_ref, w1_ref, w2_ref, a1_ref, a2_ref, g1_ref, g2_ref, seg_ref,
                    wo_ref, ko_ref, vo_ref, kko_ref, bo_ref, uo_ref, kro_ref, go_ref, bon_ref):
    xx = x_prev - x
    xw = (x + xx * mu_wag_ref[0:1, :]).astype(bf16)
    xa = (x + xx * mu_wag_ref[1:2, :]).astype(bf16)
    xg = (x + xx * mu_wag_ref[2:3, :]).astype(bf16)
    w0, a0, k_k, k_a, r_k = (vec_ref[i:i + 1, :] for i in range(5))

    lw = jnp.tanh(jnp.dot(xw, w1_ref[...], preferred_element_type=f32))
    wl = w0 + jnp.dot(lw.astype(bf16), w2_ref[...], preferred_element_type=f32)
    w_log = -_softplus(-wl) - 0.5
    decay = jnp.exp(-jnp.exp(w_log))
    la = jnp.dot(xa, a1_ref[...], preferred_element_type=f32)
    a = jax.nn.sigmoid(a0 + jnp.dot(la.astype(bf16), a2_ref[...], preferred_element_type=f32))
    lg = jax.nn.sigmoid(jnp.dot(xg, g1_ref[...], preferred_element_type=f32))
    g = jnp.dot(lg.astype(bf16), g2_ref[...], preferred_element_type=f32)

    def shifted(i):
        cur = rkv[:, GROUP * i:GROUP * (i + 1)]
        return cur + mu_rkv_ref[i:i + 1, :] * (rkv_prev[:, GROUP * i:GROUP * (i + 1)] - cur)

    r = shifted(0)
    k = shifted(1)
    v = shifted(2)

    seg = seg_ref[...]
    kk = k * k_k
    ss = _seg_dot(kk * kk, seg)
    kk = kk * lax.rsqrt(jnp.maximum(ss, 1e-24))
    k = k * (1.0 + (a - 1.0) * k_a)
    bonus = _seg_dot(r * k * r_k, seg) * v
    b = kk * a
    b_dot_r = _seg_dot(b * r, seg)
    k_dot_r = _seg_dot(k * r, seg)

    _put_blocks(wo_ref, decay)
    _put_blocks(ko_ref, k)
    _put_blocks(vo_ref, v)
    _put_blocks(kko_ref, kk)
    _put_blocks(bo_ref, b)
    _put_blocks(uo_ref, decay * r - kk * b_dot_r)
    _put_blocks(kro_ref, k_dot_r)
    _put_blocks(go_ref, g)
    _put_blocks(bon_ref, bonus)


def _rwkv_prep_seq_kernel(x_ref, r_ref, k_ref, v_ref, *rest):
    params, outs, (x_carry, rkv_carry) = rest[:10], rest[10:19], rest[19:]
    ti = pl.program_id(1)

    @pl.when(ti == 0)
    def _():
        x_carry[...] = jnp.zeros(x_carry.shape, f32)
        rkv_carry[...] = jnp.zeros(rkv_carry.shape, f32)

    x = x_ref[...]
    rkv = jnp.concatenate([_cat_blocks(r_ref), _cat_blocks(k_ref), _cat_blocks(v_ref)], axis=1)
    tm = x.shape[0]

    def prev_rows(cur, carry):
        first = lax.broadcasted_iota(jnp.int32, cur.shape, 0) == 0
        prev = jnp.where(first, carry[7:8, :], pltpu.roll(cur, 1, 0))
        carry[...] = cur[tm - 8:tm]
        return prev

    _rwkv_prep_core(x, prev_rows(x, x_carry), rkv, prev_rows(rkv, rkv_carry), *params, *outs)


def _rwkv_prep_rows_kernel(x_ref, xp_ref, r_ref, k_ref, v_ref, rp_ref, kp_ref, vp_ref, *rest):
    rkv = jnp.concatenate([_cat_blocks(r_ref), _cat_blocks(k_ref), _cat_blocks(v_ref)], axis=1)
    rkv_prev = jnp.concatenate([_cat_blocks(rp_ref), _cat_blocks(kp_ref), _cat_blocks(vp_ref)], axis=1)
    _rwkv_prep_core(x_ref[...], xp_ref[...], rkv, rkv_prev, *rest)


def _rwkv_prep_prompt(x, p, params, n_seq, t, tm):
    nt = t // tm

    def grp(gi):
        return pl.BlockSpec((3, tm, LANES), lambda b, ti: (gi, b * nt + ti, 0))

    def full(a):
        return pl.BlockSpec(a.shape, lambda b, ti: (0,) * a.ndim)

    scan_out = jax.ShapeDtypeStruct((3, t, n_seq * LANES), f32)
    tok_out = jax.ShapeDtypeStruct((3, n_seq * t, LANES), f32)
    return pl.pallas_call(
        _rwkv_prep_seq_kernel,
        grid=(n_seq, nt),
        in_specs=[pl.BlockSpec((tm, D_MODEL), lambda b, ti: (b * nt + ti, 0)), grp(0), grp(1), grp(2)]
        + [full(a) for a in params],
        out_specs=[pl.BlockSpec((3, tm, LANES), lambda b, ti: (0, ti, b))] * 7
        + [pl.BlockSpec((3, tm, LANES), lambda b, ti: (0, b * nt + ti, 0))] * 2,
        out_shape=[scan_out] * 7 + [tok_out] * 2,
        scratch_shapes=[pltpu.VMEM((8, D_MODEL), f32), pltpu.VMEM((8, 3 * GROUP), f32)],
        compiler_params=_cparams(("parallel", "arbitrary")),
        name="rwkv_prep_prompt",
    )(x, p, p, p, *params)


def _rwkv_prep_rows(x, xp, p, pp, params, row0, n_rows, tm):
    i0 = row0 // tm

    def grp(gi):
        return pl.BlockSpec((3, tm, LANES), lambda i: (gi, i0 + i, 0))

    def grp_prev(gi):
        return pl.BlockSpec((3, tm, LANES), lambda i: (gi, i, 0))

    def full(a):
        return pl.BlockSpec(a.shape, lambda i: (0,) * a.ndim)

    out = jax.ShapeDtypeStruct((3, n_rows, LANES), f32)
    return pl.pallas_call(
        _rwkv_prep_rows_kernel,
        grid=(n_rows // tm,),
        in_specs=[pl.BlockSpec((tm, D_MODEL), lambda i: (i0 + i, 0)), pl.BlockSpec((tm, D_MODEL), lambda i: (i, 0)),
                  grp(0), grp(1), grp(2), grp_prev(0), grp_prev(1), grp_prev(2)] + [full(a) for a in params],
        out_specs=[pl.BlockSpec((3, tm, LANES), lambda i: (0, i, 0))] * 9,
        out_shape=[out] * 9,
        compiler_params=_cparams(("parallel",)),
        name="rwkv_prep_rows",
    )(x, xp, p, p, p, pp, pp, pp, *params)


def _rwkv_scan_kernel(w_ref, k_ref, v_ref, kk_ref, b_ref, u_ref, kr_ref, s0_ref, seg_ref, y_ref, so_ref, s_scr,
                      *, n_steps):
    ti = pl.program_id(1)
    bb = w_ref.shape[2] // LANES
    rows = 3 * bb * HEAD_DIM

    @pl.when(ti == 0)
    def _():
        s_scr[...] = s0_ref[...].reshape(rows, LANES)

    if n_steps < y_ref.shape[1]:
        y_ref[...] = jnp.zeros(y_ref.shape, f32)
    eye = ((lax.broadcasted_iota(jnp.int32, (rows, LANES), 0) % HEAD_DIM)
           == (lax.broadcasted_iota(jnp.int32, (rows, LANES), 1) % HEAD_DIM))
    seg = seg_ref[...]

    def rows_of(ref, t):
        tiles = []
        for c in range(3):
            row = ref[c, pl.ds(t, 1), :]
            tiles += [jnp.broadcast_to(row[:, LANES * b:LANES * (b + 1)], (HEAD_DIM, LANES)) for b in range(bb)]
        return jnp.concatenate(tiles, axis=0)

    def step(t, carry):
        s = s_scr[...]
        p = (s * rows_of(kk_ref, t)).astype(bf16)
        q = (s * rows_of(u_ref, t)).astype(bf16)
        vm = jnp.where(eye, rows_of(v_ref, t), 0.0)
        vhi = vm.astype(bf16)
        vlo = (vm - vhi.astype(f32)).astype(bf16)
        red1 = jnp.dot(jnp.concatenate([p, q], axis=0), seg, preferred_element_type=f32)
        red2 = jnp.dot(jnp.concatenate([vhi, vlo], axis=0), seg, preferred_element_type=f32)
        sa = red1[0:rows]
        vcol = red2[0:rows] + red2[rows:2 * rows]
        ycol = red1[rows:2 * rows] + vcol * rows_of(kr_ref, t)
        s_scr[...] = s * rows_of(w_ref, t) - sa * rows_of(b_ref, t) + vcol * rows_of(k_ref, t)
        ym = jnp.where(eye, ycol, 0.0)
        for c in range(3):
            y_ref[c, pl.ds(t, 1), :] = jnp.concatenate(
                [jnp.sum(ym[(c * bb + b) * HEAD_DIM:(c * bb + b + 1) * HEAD_DIM], axis=0, keepdims=True)
                 for b in range(bb)], axis=1)
        return carry

    lax.fori_loop(0, n_steps, step, 0)

    @pl.when(ti == pl.num_programs(1) - 1)
    def _():
        so_ref[...] = s_scr[...].reshape(so_ref.shape)


def _rwkv_scan(ops, s0, seg, bb, tc, n_steps):
    _, t, lanes = ops[0].shape
    b = lanes // LANES
    blk = pl.BlockSpec((3, tc, bb * LANES), lambda bi, ti: (0, ti, bi))
    st = pl.BlockSpec((3, bb, HEAD_DIM, LANES), lambda bi, ti: (0, bi, 0, 0))
    return pl.pallas_call(
        functools.partial(_rwkv_scan_kernel, n_steps=n_steps),
        grid=(b // bb, t // tc),
        in_specs=[blk] * 7 + [st, pl.BlockSpec((LANES, LANES), lambda bi, ti: (0, 0))],
        out_specs=[blk, st],
        out_shape=[jax.ShapeDtypeStruct((3, t, b * LANES), f32),
                   jax.ShapeDtypeStruct((3, b, HEAD_DIM, LANES), f32)],
        scratch_shapes=[pltpu.VMEM((3 * bb * HEAD_DIM, LANES), f32)],
        compiler_params=_cparams(("parallel", "arbitrary")),
        name="rwkv_scan",
    )(*ops, s0, seg)


def _log_gamma(h):
    return float(np.log(np.float32(1.0) - np.float32(2.0) ** np.float32(-5.0 - min(h, H_RET - 1))))


def _seg_dot(x, seg16):
    hi = x.astype(bf16)
    lo = (x - hi.astype(f32)).astype(bf16)
    return (jnp.dot(hi, seg16, preferred_element_type=f32) + jnp.dot(lo, seg16, preferred_element_type=f32))


def _retention_kernel(q_ref, k_ref, v_ref, g_ref, s0_ref, seg_ref, y_ref, so_ref, s_scr, *, rows, n_valid):
    ci = pl.program_id(1)
    n_seq = q_ref.shape[1] // rows

    @pl.when(ci == 0)
    def _():
        s_scr[...] = s0_ref[...]

    lo = _lane_lo((rows, LANES))
    ii = lax.broadcasted_iota(jnp.int32, (rows, rows), 0)
    jj = lax.broadcasted_iota(jnp.int32, (rows, rows), 1)
    rel = (ii - jj).astype(f32)
    causal = ii >= jj
    pos = lax.broadcasted_iota(jnp.int32, (rows, LANES), 0)
    posf = pos.astype(f32)
    row_ok = pos < n_valid
    sq_r = lax.broadcasted_iota(jnp.int32, (LANES, LANES), 0) < HEAD_DIM
    sq_c = lax.broadcasted_iota(jnp.int32, (LANES, LANES), 1) < HEAD_DIM
    same_head = sq_r == sq_c
    seg = seg_ref[...]
    items = [(sq, c) for sq in range(n_seq) for c in range(3)]
    decays = {}
    for c in range(3):
        lg0, lg1 = _log_gamma(2 * c), _log_gamma(2 * c + 1)
        lg_lane = jnp.where(lo, lg0, lg1)
        decays[c] = dict(
            dmat=[jnp.where(causal, jnp.exp(lg * jnp.maximum(rel, 0.0)), 0.0) for lg in (lg0, lg1)],
            cross=jnp.exp(lg_lane * (posf + 1.0)),
            k_in=jnp.exp(lg_lane * (float(n_valid) - 1.0 - posf)),
            state=jnp.exp(jnp.where(sq_c, lg0, lg1) * float(n_valid)))

    st1 = {}
    for sq, c in items:
        sl = slice(sq * rows, (sq + 1) * rows)
        q = q_ref[c, sl, :]
        k = jnp.where(row_ok, k_ref[c, sl, :], 0.0)
        k16 = k.astype(bf16)
        v16 = v_ref[c, sl, :].astype(bf16)
        scores = [lax.dot_general(jnp.where(lo if half == 0 else jnp.logical_not(lo), q, 0.0).astype(bf16), k16, _NT,
                                  preferred_element_type=f32) for half in range(2)]
        state = s_scr[sq, c]
        cross = jnp.dot(q.astype(bf16), state.astype(bf16), preferred_element_type=f32)
        upd = lax.dot_general((k * decays[c]["k_in"]).astype(bf16), v16, _TN, preferred_element_type=f32)
        s_scr[sq, c] = state * decays[c]["state"] + jnp.where(same_head, upd, 0.0)
        st1[sq, c] = (scores, cross, v16)

    outs = {}
    for sq, c in items:
        scores, cross, v16 = st1[sq, c]
        o_halves = [jnp.dot((scores[half] * decays[c]["dmat"][half]).astype(bf16), v16, preferred_element_type=f32)
                    for half in range(2)]
        outs[sq, c] = jnp.where(lo, o_halves[0], o_halves[1]) + cross * decays[c]["cross"]

    devs = {it: outs[it] - _seg_dot(outs[it], seg) for it in items}
    for sq, c in items:
        d = devs[sq, c]
        yn = d * lax.rsqrt(_seg_dot(d * d, seg) + RET_GN_EPS)
        g = g_ref[c, sq * rows:(sq + 1) * rows, :]
        y_ref[c, sq * rows:(sq + 1) * rows, :] = g * jax.nn.sigmoid(g) * yn

    @pl.when(ci == pl.num_programs(1) - 1)
    def _():
        so_ref[...] = s_scr[...]


def _retention(p, s0_pairs, seg16, n_seq, seq_blk, rows, n_chunks, n_valid, row0):
    assert seq_blk == 1 or n_chunks == 1
    blk_rows = seq_blk * rows

    def grp(gi):
        return pl.BlockSpec((3, blk_rows, LANES), lambda b, c: (gi, row0 // blk_rows + b * n_chunks + c, 0))

    st = pl.BlockSpec((seq_blk, 3, LANES, LANES), lambda b, c: (b, 0, 0, 0))
    return pl.pallas_call(
        functools.partial(_retention_kernel, rows=rows, n_valid=n_valid),
        grid=(n_seq // seq_blk, n_chunks),
        in_specs=[grp(3), grp(4), grp(5), grp(6), st, pl.BlockSpec(seg16.shape, lambda b, c: (0, 0))],
        out_specs=[pl.BlockSpec((3, blk_rows, LANES), lambda b, c: (0, b * n_chunks + c, 0)), st],
        out_shape=[jax.ShapeDtypeStruct((3, n_seq * n_chunks * rows, LANES), f32),
                   jax.ShapeDtypeStruct((n_seq, 3, LANES, LANES), f32)],
        scratch_shapes=[pltpu.VMEM((seq_blk, 3, LANES, LANES), f32)],
        compiler_params=_cparams(("parallel", "arbitrary")),
        name="retention",
    )(p, p, p, p, s0_pairs, seg16)


def _dil_prompt_kernel(q_ref, k_ref, v_ref, y_ref, o_scr, l_scr):
    i = pl.program_id(1)
    n_steps = pl.num_programs(1)
    blk = DIL_BLK
    lo = _lane_lo((blk, LANES))
    ii = lax.broadcasted_iota(jnp.int32, (blk, 2 * blk), 0)
    jj = lax.broadcasted_iota(jnp.int32, (blk, 2 * blk), 1)
    band = jnp.logical_and(jj >= ii, jj <= ii + blk)

    for p, dil in enumerate(DILATIONS):
        nb = q_ref.shape[1] // (dil * blk)
        r = i // nb
        cb = i % nb
        start_q = r + dil * blk * cb
        start_p = r + dil * blk * jnp.maximum(cb - 1, 0)
        valid = jnp.logical_and(band, jnp.logical_or(cb > 0, jj >= blk))

        def rows(ref, c, start):
            if dil == 1:
                return ref[c, pl.ds(start, blk), :]
            return ref[c, pl.ds(start, blk, stride=dil), :]

        for c in range(3):
            q = rows(q_ref, c, start_q)
            kwin = jnp.concatenate([rows(k_ref, c, start_p), rows(k_ref, c, start_q)], axis=0).astype(bf16)
            vwin = jnp.concatenate([rows(v_ref, c, start_p), rows(v_ref, c, start_q)], axis=0).astype(bf16)
            outs, lses = [], []
            for half in range(2):
                qm = jnp.where(lo if half == 0 else jnp.logical_not(lo), q, 0.0).astype(bf16)
                s = lax.dot_general(qm, kwin, _NT, preferred_element_type=f32)
                s = jnp.where(valid, s, -jnp.inf)
                mx = jnp.max(s, axis=-1, keepdims=True)
                lse = mx + jnp.log(jnp.sum(jnp.exp(s - mx), axis=-1, keepdims=True))
                prob = jnp.exp(s - lse).astype(bf16)
                outs.append(jnp.dot(prob, vwin, preferred_element_type=f32))
                lses.append(lse)
            o_pair = jnp.where(lo, outs[0], outs[1])
            l_pair = jnp.where(lo, lses[0], lses[1])
            if dil == 1:
                o_scr[p, c, pl.ds(start_q, blk), :] = o_pair
                l_scr[p, c, pl.ds(start_q, blk), :] = l_pair
            else:
                o_scr[p, c, pl.ds(start_q, blk, stride=dil), :] = o_pair
                l_scr[p, c, pl.ds(start_q, blk, stride=dil), :] = l_pair

    @pl.when(i == n_steps - 1)
    def _():
        def merge(j, carry):
            sl = pl.ds(pl.multiple_of(j * blk, blk), blk)
            for c in range(3):
                ls = [l_scr[p, c, sl, :] for p in range(3)]
                mx = jnp.maximum(jnp.maximum(ls[0], ls[1]), ls[2])
                es = [jnp.exp(l - mx) for l in ls]
                tot = es[0] + es[1] + es[2]
                acc = (es[0] / tot) * o_scr[0, c, sl, :]
                acc = acc + (es[1] / tot) * o_scr[1, c, sl, :]
                acc = acc + (es[2] / tot) * o_scr[2, c, sl, :]
                y_ref[c, sl, :] = acc
            return carry

        lax.fori_loop(0, q_ref.shape[1] // blk, merge, 0)


def _dil_prompt(p, n_seq, t):
    def grp(gi):
        return pl.BlockSpec((3, t, LANES), lambda b, i: (gi, b, 0))

    n_steps = t // DIL_BLK
    return pl.pallas_call(
        _dil_prompt_kernel,
        grid=(n_seq, n_steps),
        in_specs=[grp(7), grp(8), grp(9)],
        out_specs=pl.BlockSpec((3, t, LANES), lambda b, i: (0, b, 0)),
        out_shape=jax.ShapeDtypeStruct((3, n_seq * t, LANES), f32),
        scratch_shapes=[pltpu.VMEM((3, 3, t, LANES), f32), pltpu.VMEM((3, 3, t, LANES), f32)],
        compiler_params=_cparams(("parallel", "arbitrary")),
        name="dil_prompt",
    )(p, p, p)


N_NEW = 4


def _dil_sample_kernel(q_ref, kn_ref, vn_ref, kc_ref, vc_ref, *rest, n_buf, aliased):
    y_ref, ko_ref, vo_ref = rest[2:] if aliased else rest
    t_all = n_buf + LANES
    zpad = jnp.zeros((LANES - DEC_PAD, LANES), f32)
    qi = lax.broadcasted_iota(jnp.int32, (DEC_PAD, t_all), 0)
    tok = lax.broadcasted_iota(jnp.int32, (DEC_PAD, t_all), 1)
    dist = n_buf + qi - tok
    masks = [jnp.logical_and(jnp.logical_and(dist >= 0, dist <= DIL_BLK * dil), (dist & (dil - 1)) == 0)
             for dil in DILATIONS]
    row_ok = lax.broadcasted_iota(jnp.int32, (DEC_PAD, LANES), 0) < N_NEW
    tail_ok = lax.broadcasted_iota(jnp.int32, (HEAD_DIM, LANES), 1) < N_NEW

    for c in range(3):
        kn_t = jnp.concatenate([kn_ref[c], zpad], axis=0).T
        vn_t = jnp.concatenate([vn_ref[c], zpad], axis=0).T
        q_pair = q_ref[c]
        y_halves = []
        for half in range(2):
            h = 2 * c + half
            k_tail = jnp.where(tail_ok, kn_t[HEAD_DIM * half:HEAD_DIM * (half + 1)], 0.0)
            v_tail = jnp.where(tail_ok, vn_t[HEAD_DIM * half:HEAD_DIM * (half + 1)], 0.0)
            k_all = jnp.concatenate([kc_ref[0, 0, h], k_tail], axis=1)
            v_all = jnp.concatenate([vc_ref[0, 0, h], v_tail], axis=1)
            ko_ref[0, 0, h] = pltpu.roll(k_all, t_all - N_NEW, 1)[:, :n_buf]
            vo_ref[0, 0, h] = pltpu.roll(v_all, t_all - N_NEW, 1)[:, :n_buf]
            q16 = q_pair[:, HEAD_DIM * half:HEAD_DIM * (half + 1)].astype(bf16)
            s = jnp.dot(q16, k_all.astype(bf16), preferred_element_type=f32)
            probs, lses = [], []
            for msk in masks:
                sm = jnp.where(msk, s, -jnp.inf)
                mx = jnp.max(sm, axis=-1, keepdims=True)
                lse = mx + jnp.log(jnp.sum(jnp.exp(sm - mx), axis=-1, keepdims=True))
                probs.append(jnp.exp(sm - lse).astype(bf16))
                lses.append(lse)
            o = lax.dot_general(jnp.concatenate(probs, axis=0), v_all.astype(bf16), _NT,
                                preferred_element_type=f32)
            mx = jnp.maximum(jnp.maximum(lses[0], lses[1]), lses[2])
            es = [jnp.exp(l - mx) for l in lses]
            tot = es[0] + es[1] + es[2]
            y_halves.append((es[0] / tot) * o[0:DEC_PAD] + (es[1] / tot) * o[DEC_PAD:2 * DEC_PAD]
                            + (es[2] / tot) * o[2 * DEC_PAD:3 * DEC_PAD])
        y_ref[c] = jnp.where(row_ok, jnp.concatenate(y_halves, axis=1), 0.0)


def _dil_sample(p, cache_k_t, cache_v_t, layer, prev_outs, n_seq, row0):
    n_buf = cache_k_t.shape[-1]

    def grp(gi):
        return pl.BlockSpec((3, DEC_PAD, LANES), lambda b: (gi, row0 // DEC_PAD + b, 0))

    buf = pl.BlockSpec((1, 1, H_DIL, HEAD_DIM, n_buf), lambda b: (layer, b, 0, 0, 0))
    aliased = prev_outs is not None
    extra = list(prev_outs) if aliased else []
    return pl.pallas_call(
        functools.partial(_dil_sample_kernel, n_buf=n_buf, aliased=aliased),
        grid=(n_seq,),
        in_specs=[grp(7), grp(8), grp(9), buf, buf] + [pl.BlockSpec(memory_space=pl.ANY)] * len(extra),
        out_specs=[pl.BlockSpec((3, DEC_PAD, LANES), lambda b: (0, b, 0)), buf, buf],
        out_shape=[jax.ShapeDtypeStruct((3, n_seq * DEC_PAD, LANES), f32),
                   jax.ShapeDtypeStruct(cache_k_t.shape, f32),
                   jax.ShapeDtypeStruct(cache_v_t.shape, f32)],
        input_output_aliases={5: 1, 6: 2} if aliased else {},
        compiler_params=_cparams(("arbitrary",)),
        name="dil_sample",
    )(p, p, p, cache_k_t, cache_v_t, *extra)


def _layer_norm(z, gamma, beta):
    mu = jnp.mean(z, axis=-1, keepdims=True)
    d = z - mu
    var = jnp.mean(d * d, axis=-1, keepdims=True)
    return d * lax.rsqrt(var + LN_EPS) * gamma + beta


def _merge_kernel(ya_ref, bon_ref, g_ref, yb_ref, yc_ref, x_ref, lnx_ref, seg_ref, wo_ref, ln_ref, *rest):
    o_ref = rest[-1]
    yr = _cat_blocks(ya_ref)
    seg = seg_ref[...]
    mean = _seg_dot(yr, seg)
    d = yr - mean
    var = _seg_dot(d * d, seg)
    ya = d * lax.rsqrt(var + RWKV_GN_EPS) * lnx_ref[0:1, :] + lnx_ref[1:2, :]
    ya = (ya + _cat_blocks(bon_ref)) * _cat_blocks(g_ref)
    ycat = jnp.concatenate([ya, _cat_blocks(yb_ref), _cat_blocks(yc_ref)], axis=1).astype(bf16)
    h = jnp.dot(ycat, wo_ref[...], preferred_element_type=f32)
    o_ref[...] = _layer_norm(DEEPNORM_ALPHA * x_ref[...] + h, ln_ref[0:1, :], ln_ref[1:2, :])


def _merge_prompt(ya_tm, bonus, g, yb, yc, x, params, n_seq, t, tm):
    nt = t // tm
    tok = pl.BlockSpec((3, tm, LANES), lambda b, ti: (0, b * nt + ti, 0))
    rows = pl.BlockSpec((tm, D_MODEL), lambda b, ti: (b * nt + ti, 0))

    def full(a):
        return pl.BlockSpec(a.shape, lambda b, ti: (0,) * a.ndim)

    return pl.pallas_call(
        _merge_kernel,
        grid=(n_seq, nt),
        in_specs=[pl.BlockSpec((3, tm, LANES), lambda b, ti: (0, ti, b)), tok, tok, tok, tok, rows]
        + [full(a) for a in params],
        out_specs=rows,
        out_shape=jax.ShapeDtypeStruct(x.shape, f32),
        compiler_params=_cparams(("parallel", "parallel")),
        name="mixer_merge_prompt",
    )(ya_tm, bonus, g, yb, yc, x, *params)


def _merge_rows(ya, bonus, g, yb, yc, x, params, x_out, row0, tm):
    i0 = row0 // tm
    blk = pl.BlockSpec((3, tm, LANES), lambda i: (0, i, 0))
    rows = pl.BlockSpec((tm, D_MODEL), lambda i: (i0 + i, 0))

    def full(a):
        return pl.BlockSpec(a.shape, lambda i: (0,) * a.ndim)

    return pl.pallas_call(
        _merge_kernel,
        grid=((x.shape[0] - row0) // tm,),
        in_specs=[blk] * 5 + [rows] + [full(a) for a in params] + [pl.BlockSpec(memory_space=pl.ANY)],
        out_specs=rows,
        out_shape=jax.ShapeDtypeStruct(x.shape, f32),
        input_output_aliases={6 + len(params): 0},
        compiler_params=_cparams(("parallel",)),
        name="mixer_merge_rows",
    )(ya, bonus, g, yb, yc, x, *params, x_out)


EXPERT_BLOCK = 8


def _first_max(x, idx, axes, sentinel):
    mx = x
    for ax in axes:
        mx = jnp.max(mx, axis=ax, keepdims=True)
    am = jnp.where(x == mx, idx, sentinel)
    for ax in axes:
        am = jnp.min(am, axis=ax, keepdims=True)
    return mx, am


def _route(logits_t, bias_col):
    n_tok = logits_t.shape[1]
    per = N_EXPERTS // N_ROUTE_GROUPS
    scores = jax.nn.sigmoid(logits_t)
    shp = (N_ROUTE_GROUPS, per, n_tok)
    s3 = scores.reshape(shp)
    b3 = (scores + bias_col).reshape(shp)
    sub = lax.broadcasted_iota(jnp.int32, shp, 1)
    grp = lax.broadcasted_iota(jnp.int32, shp, 0)
    eid = grp * per + sub
    neg = -jnp.inf

    m1, i1 = _first_max(b3, sub, (1,), per)
    m2, _ = _first_max(jnp.where(sub == i1, neg, b3), sub, (1,), per)
    gscore = m1 + m2
    gid = lax.broadcasted_iota(jnp.int32, gscore.shape, 0)
    gsel = jnp.zeros(gscore.shape, jnp.bool_)
    for _ in range(TOPK_GROUPS):
        _, gi = _first_max(gscore, gid, (0,), N_ROUTE_GROUPS)
        hit = gid == gi
        gsel = jnp.logical_or(gsel, hit)
        gscore = jnp.where(hit, neg, gscore)
    masked = jnp.where(gsel, b3, neg)
    esel = jnp.zeros(shp, jnp.bool_)
    for _ in range(TOP_K):
        _, ei = _first_max(masked, eid, (1, 0), N_EXPERTS)
        hit = eid == ei
        esel = jnp.logical_or(esel, hit)
        masked = jnp.where(hit, neg, masked)
    top_s = jnp.where(esel, s3, 0.0)
    denom = jnp.sum(jnp.sum(top_s, axis=1, keepdims=True), axis=0, keepdims=True)
    return (top_s / denom * ROUTE_SCALE).reshape(N_EXPERTS, n_tok)


def _moe_kernel(x_ref, wr_ref, rb_ref, wg_ref, wu_ref, wd_ref, wsg_ref, wsu_ref, wsd_ref, ln_ref,
                o_ref, x16_scr, comb_scr, acc_scr):
    j = pl.program_id(1)

    @pl.when(j == 0)
    def _():
        x16 = x_ref[...].astype(bf16)
        x16_scr[...] = x16
        logits_t = lax.dot_general(wr_ref[...], x16, _NT, preferred_element_type=f32)
        comb_t = _route(logits_t, rb_ref[...])
        comb_t = jnp.concatenate([comb_t, jnp.zeros_like(comb_t)], axis=0)
        comb = comb_t.T
        comb_scr[0] = comb
        for jj in range(1, N_EXPERTS // EXPERT_BLOCK):
            comb_scr[jj] = pltpu.roll(comb, LANES - EXPERT_BLOCK * jj, 1)
        sg = jnp.dot(x16, wsg_ref[...], preferred_element_type=f32)
        su = jnp.dot(x16, wsu_ref[...], preferred_element_type=f32)
        hs = (sg * jax.nn.sigmoid(sg) * su).astype(bf16)
        acc_scr[...] = jnp.dot(hs, wsd_ref[...], preferred_element_type=f32)

    x16 = x16_scr[...]
    gate = jnp.dot(x16, wg_ref[...], preferred_element_type=f32)
    up = jnp.dot(x16, wu_ref[...], preferred_element_type=f32)
    comb = comb_scr[j]
    tm = comb.shape[0]
    cexp = jnp.concatenate([jnp.broadcast_to(comb[:, e:e + 1], (tm, D_EXPERT)) for e in range(EXPERT_BLOCK)], axis=1)
    h = (gate * jax.nn.sigmoid(gate) * up * cexp).astype(bf16)
    acc_scr[...] += jnp.dot(h, wd_ref[...], preferred_element_type=f32)

    @pl.when(j == pl.num_programs(1) - 1)
    def _():
        o_ref[...] = _layer_norm(DEEPNORM_ALPHA * x_ref[...] + acc_scr[...], ln_ref[0:1, :], ln_ref[1:2, :])


def _moe(x, wr_t, rb_col, wg, wu, wd, wsg, wsu, wsd, ln, tm):
    m = x.shape[0]
    nb = EXPERT_BLOCK * D_EXPERT

    def full(a):
        return pl.BlockSpec(a.shape, lambda i, j: (0,) * a.ndim)

    return pl.pallas_call(
        _moe_kernel,
        grid=(m // tm, N_EXPERTS // EXPERT_BLOCK),
        in_specs=[pl.BlockSpec((tm, D_MODEL), lambda i, j: (i, 0)), full(wr_t), full(rb_col),
                  pl.BlockSpec((D_MODEL, nb), lambda i, j: (0, j)),
                  pl.BlockSpec((D_MODEL, nb), lambda i, j: (0, j)),
                  pl.BlockSpec((nb, D_MODEL), lambda i, j: (j, 0)),
                  full(wsg), full(wsu), full(wsd), full(ln)],
        out_specs=pl.BlockSpec((tm, D_MODEL), lambda i, j: (i, 0)),
        out_shape=jax.ShapeDtypeStruct((m, D_MODEL), f32),
        scratch_shapes=[pltpu.VMEM((tm, D_MODEL), bf16),
                        pltpu.VMEM((N_EXPERTS // EXPERT_BLOCK, tm, LANES), f32),
                        pltpu.VMEM((tm, D_MODEL), f32)],
        compiler_params=_cparams(("parallel", "arbitrary")),
        name="moe",
    )(x, wr_t, rb_col, wg, wu, wd, wsg, wsu, wsd, ln)


def _pad_heads(a, n_heads):
    return jnp.pad(a, [(0, 0)] * (a.ndim - 1) + [(0, GROUP - n_heads * HEAD_DIM)])


def _pad_w_in(w):
    widths = [C_RWKV] * 3 + [C_RET] * 4 + [C_DIL] * 3
    parts, off = [], 0
    for wd in widths:
        parts.append(jnp.pad(w[:, off:off + wd], ((0, 0), (0, GROUP - wd))))
        off += wd
    return jnp.concatenate(parts, axis=1).astype(bf16)


def _rope_tables(pos, inv_freq):
    ang = pos.astype(f32)[:, None] * inv_freq[None, :]
    cos, sin = jnp.cos(ang), jnp.sin(ang)
    return jnp.tile(jnp.concatenate([cos, cos], axis=1), (1, 2)), jnp.tile(jnp.concatenate([-sin, sin], axis=1), (1, 2))


def _segment_matrix(width, scale):
    idx = np.arange(width) // HEAD_DIM
    return jnp.asarray((idx[:, None] == idx[None, :]).astype(np.float32) * scale)


def _pairs_from_heads(s):
    b = s.shape[0]
    s6 = jnp.pad(s, ((0, 0), (0, 1), (0, 0), (0, 0))).reshape(b, 3, 2, HEAD_DIM, HEAD_DIM)
    z = jnp.zeros((b, 3, HEAD_DIM, HEAD_DIM), s.dtype)
    top = jnp.concatenate([s6[:, :, 0], z], axis=-1)
    bot = jnp.concatenate([z, s6[:, :, 1]], axis=-1)
    return jnp.concatenate([top, bot], axis=-2)


def _heads_from_pairs(sp):
    b = sp.shape[0]
    h0 = sp[:, :, :HEAD_DIM, :HEAD_DIM]
    h1 = sp[:, :, HEAD_DIM:, HEAD_DIM:]
    return jnp.stack([h0, h1], axis=2).reshape(b, 6, HEAD_DIM, HEAD_DIM)[:, :H_RET]


def _rwkv_pairs_from_heads(s):
    b = s.shape[0]
    s6 = jnp.pad(s, ((0, 0), (0, 1), (0, 0), (0, 0))).reshape(b, 3, 2, HEAD_DIM, HEAD_DIM)
    return jnp.transpose(s6, (1, 0, 3, 2, 4)).reshape(3, b, HEAD_DIM, LANES)


def _rwkv_heads_from_pairs(sp):
    b = sp.shape[1]
    s6 = jnp.transpose(sp.reshape(3, b, HEAD_DIM, 2, HEAD_DIM), (1, 0, 3, 2, 4))
    return s6.reshape(b, 6, HEAD_DIM, HEAD_DIM)[:, :H_RWKV]


def kernel(x_prompt, x_sample, state_rwkv, state_shift, state_ret, cache_k_win, cache_v_win, w_in, mu_rkv, mu_wag, w0, w1, w2, a0, a1, a2, g1, g2, k_k, k_a, r_k, lnx_g, lnx_b, w_out, ln1_g, ln1_b, w_router, router_bias, w_gate, w_up, w_down, ws_gate, ws_up, ws_down, ln2_g, ln2_b):
    bp, tp, _ = x_prompt.shape
    bs, ts, _ = x_sample.shape
    n_buf = cache_k_win.shape[2]
    depth = w_in.shape[0]
    assert ts == N_NEW and n_buf == DILATIONS[-1] * DIL_BLK and tp % (DILATIONS[-1] * DIL_BLK) == 0
    mp = bp * tp
    ms = bs * DEC_PAD
    m = mp + ms
    tm = 256
    tm_moe = 512
    assert m % tm_moe == 0 and bs % 8 == 0

    pos = jnp.concatenate([jnp.tile(jnp.arange(tp), bp), jnp.tile(PAST_LEN + jnp.arange(DEC_PAD), bs)])
    inv_ret = 1.0 / (10000.0 ** jnp.linspace(0.0, 1.0, HEAD_DIM // 2, dtype=f32))
    inv_dil = ROPE_THETA ** (-jnp.arange(0, HEAD_DIM, 2, dtype=f32) / HEAD_DIM)
    cr, sr = _rope_tables(pos, inv_ret)
    cd, sd = _rope_tables(pos, inv_dil)
    zero_tab = jnp.zeros((bs, LANES), f32)

    seg_sum = _segment_matrix(GROUP, 1.0).astype(bf16)
    seg_mean = _segment_matrix(GROUP, 1.0 / HEAD_DIM).astype(bf16)
    seg_mean_pair = _segment_matrix(LANES, 1.0 / HEAD_DIM).astype(bf16)
    seg_pair_bf16 = _segment_matrix(LANES, 1.0).astype(bf16)

    xs_pad = jnp.pad(x_sample, ((0, 0), (0, DEC_PAD - ts), (0, 0)))
    x = jnp.concatenate([x_prompt.reshape(mp, D_MODEL), xs_pad.reshape(ms, D_MODEL)], axis=0)

    cache_k_t = jnp.transpose(cache_k_win, (0, 1, 3, 4, 2))
    cache_v_t = jnp.transpose(cache_v_win, (0, 1, 3, 4, 2))
    win_outs = None

    outs_p = [[] for _ in range(5)]
    outs_s = [[] for _ in range(3)]
    for l in range(depth):
        xp3 = x[:mp].reshape(bp, tp, D_MODEL)
        xs3 = x[mp:].reshape(bs, DEC_PAD, D_MODEL)
        x_last_s = state_shift[l]
        outs_p[1].append(xp3[:, -1])
        outs_s[1].append(xs3[:, ts - 1])

        w_in_p = _pad_w_in(w_in[l])
        p, k_new, v_new = _inproj(x, w_in_p, cr, sr, cd, sd, tm)
        p_last, _, _ = _inproj(x_last_s, w_in_p, zero_tab, zero_tab, zero_tab, zero_tab, min(bs, 128))

        xs_prev = jnp.concatenate([x_last_s[:, None], xs3[:, :-1]], axis=1).reshape(ms, D_MODEL)
        pr_s = p[:9, mp:].reshape(9, bs, DEC_PAD, LANES)
        pp_s = jnp.concatenate([p_last[:9][:, :, None], pr_s[:, :, :-1]], axis=2).reshape(9, ms, LANES)

        vec = jnp.stack([_pad_heads(v, H_RWKV) for v in (w0[l], a0[l], k_k[l], k_a[l], r_k[l].reshape(-1))])
        prep_params = (_pad_heads(mu_rkv[l], H_RWKV), mu_wag[l], vec,
                       w1[l].astype(bf16), _pad_heads(w2[l], H_RWKV).astype(bf16),
                       a1[l].astype(bf16), _pad_heads(a2[l], H_RWKV).astype(bf16),
                       g1[l].astype(bf16), _pad_heads(g2[l], H_RWKV).astype(bf16), seg_sum)
        prep_p = _rwkv_prep_prompt(x, p, prep_params, bp, tp, tm)
        prep_s = _rwkv_prep_rows(x, xs_prev, p, pp_s, prep_params, mp, ms, tm)

        ops_s = [jnp.swapaxes(a.reshape(3, bs, DEC_PAD, LANES), 1, 2).reshape(3, DEC_PAD, bs * LANES)
                 for a in prep_s[:7]]
        ya_p, s_rwkv_p = _rwkv_scan(list(prep_p[:7]), jnp.zeros((3, bp, HEAD_DIM, LANES), f32), seg_pair_bf16,
                                    bp, 64, 64)
        ya_s, s_rwkv_s = _rwkv_scan(ops_s, _rwkv_pairs_from_heads(state_rwkv[l]), seg_pair_bf16, 8, DEC_PAD, ts)
        ya_s = jnp.swapaxes(ya_s.reshape(3, DEC_PAD, bs, LANES), 1, 2).reshape(3, ms, LANES)
        outs_p[0].append(_rwkv_heads_from_pairs(s_rwkv_p))
        outs_s[0].append(_rwkv_heads_from_pairs(s_rwkv_s))

        yb_p, s_ret_p = _retention(p, jnp.zeros((bp, 3, LANES, LANES), f32), seg_mean_pair,
                                   bp, 1, RET_CHUNK, tp // RET_CHUNK, RET_CHUNK, 0)
        yb_s, s_ret_s = _retention(p, _pairs_from_heads(state_ret[l]), seg_mean_pair, bs, 8, DEC_PAD, 1, ts, mp)
        outs_p[2].append(_heads_from_pairs(s_ret_p))
        outs_s[2].append(_heads_from_pairs(s_ret_s))

        yc_p = _dil_prompt(p, bp, tp)
        yc_s, k_win_t, v_win_t = _dil_sample(p, cache_k_t, cache_v_t, l, win_outs, bs, mp)
        win_outs = (k_win_t, v_win_t)
        keep = min(n_buf, tp)
        outs_p[3].append(k_new[:mp].reshape(bp, tp, H_DIL, HEAD_DIM)[:, tp - keep:])
        outs_p[4].append(v_new[:mp].reshape(bp, tp, H_DIL, HEAD_DIM)[:, tp - keep:])

        w_out_p = jnp.concatenate([
            jnp.pad(w_out[l][:C_RWKV], ((0, GROUP - C_RWKV), (0, 0))),
            jnp.pad(w_out[l][C_RWKV:C_RWKV + C_RET], ((0, GROUP - C_RET), (0, 0))),
            w_out[l][C_RWKV + C_RET:]], axis=0).astype(bf16)
        lnx = jnp.stack([_pad_heads(lnx_g[l], H_RWKV), _pad_heads(lnx_b[l], H_RWKV)])
        merge_params = (lnx, seg_mean, w_out_p, jnp.stack([ln1_g[l], ln1_b[l]]))
        x_new = _merge_prompt(ya_p, prep_p[8], prep_p[7], yb_p, yc_p, x, merge_params, bp, tp, tm)
        x = _merge_rows(ya_s, prep_s[8], prep_s[7], yb_s, yc_s, x, merge_params, x_new, mp, tm)

        wg = jnp.transpose(w_gate[l], (1, 0, 2)).reshape(D_MODEL, N_EXPERTS * D_EXPERT).astype(bf16)
        wu = jnp.transpose(w_up[l], (1, 0, 2)).reshape(D_MODEL, N_EXPERTS * D_EXPERT).astype(bf16)
        wd = w_down[l].reshape(N_EXPERTS * D_EXPERT, D_MODEL).astype(bf16)
        x = _moe(x, w_router[l].T.astype(bf16), router_bias[l].reshape(N_EXPERTS, 1), wg, wu, wd,
                 ws_gate[l].astype(bf16), ws_up[l].astype(bf16), ws_down[l].astype(bf16),
                 jnp.stack([ln2_g[l], ln2_b[l]]), tm_moe)

    y_prompt = x[:mp].reshape(bp, tp, D_MODEL)
    y_sample = x[mp:].reshape(bs, DEC_PAD, D_MODEL)[:, :ts]
    wins = tuple(jnp.transpose(w, (0, 1, 4, 2, 3)) for w in win_outs)
    return ((y_prompt, y_sample) + tuple(jnp.stack(o) for o in outs_p)
            + tuple(jnp.stack(o) for o in outs_s) + wins)
```

```python
import functools
import math

import numpy as np
import jax
import jax.numpy as jnp
from jax import lax
from jax.experimental import pallas as pl
from jax.experimental.pallas import tpu as pltpu

f32 = jnp.float32
bf16 = jnp.bfloat16

D_MODEL = 1024
HEAD_DIM = 64
LANES = 128
H_RWKV = 5
H_RET = 5
H_DIL = 6
GROUP = 384
N_GROUPS_IN = 10
N_BLOCKS_IN = 3 * N_GROUPS_IN
C_RWKV = H_RWKV * HEAD_DIM
C_RET = H_RET * HEAD_DIM
C_DIL = H_DIL * HEAD_DIM
PAST_LEN = 2048
DEC_PAD = 8
RET_CHUNK = 128
RWKV_GN_EPS = 64e-5
RET_GN_EPS = 1e-5
DILATIONS = (1, 4, 16)
DIL_BLK = 128
ROPE_THETA = 10000.0
N_EXPERTS = 64
D_EXPERT = 128
TOP_K = 8
N_ROUTE_GROUPS = 8
TOPK_GROUPS = 4
ROUTE_SCALE = 2.5
LN_EPS = 1e-5
DEPTH = 2
DEEPNORM_ALPHA = (2 * DEPTH) ** 0.25
VMEM_LIMIT = 56 * 1024 * 1024
HIGHEST = lax.Precision.HIGHEST

_NT = (((1,), (1,)), ((), ()))
_TN = (((0,), (0,)), ((), ()))


def _cparams(sem):
    return pltpu.CompilerParams(dimension_semantics=sem, vmem_limit_bytes=VMEM_LIMIT)


def _lane_lo(shape):
    return (lax.broadcasted_iota(jnp.int32, shape, len(shape) - 1) % LANES) < HEAD_DIM


def _cat_blocks(ref):
    return jnp.concatenate([ref[0], ref[1], ref[2]], axis=1)


def _put_blocks(ref, val):
    for c in range(3):
        ref[c] = val[:, LANES * c:LANES * (c + 1)]


def _inproj_kernel(x_ref, w_ref, cr_ref, sr_ref, cd_ref, sd_ref, p_ref, *win_refs):
    acc = jnp.dot(x_ref[...].astype(bf16), w_ref[...], preferred_element_type=f32)
    tm = acc.shape[0]
    first = (lax.broadcasted_iota(jnp.int32, (tm, LANES), 1) % HEAD_DIM) < HEAD_DIM // 2

    def rot(xb, c, s):
        partner = jnp.where(first, pltpu.roll(xb, LANES - HEAD_DIM // 2, 1), pltpu.roll(xb, HEAD_DIM // 2, 1))
        return xb * c + partner * s

    for j in range(N_BLOCKS_IN):
        blk = acc[:, LANES * j:LANES * (j + 1)]
        g = j // 3
        if g in (3, 4):
            blk = rot(blk, cr_ref[...], sr_ref[...])
        if g in (7, 8):
            blk = rot(blk, cd_ref[...], sd_ref[...])
        if g in (4, 7):
            blk = blk * (HEAD_DIM ** -0.5)
        p_ref[j] = blk
        if win_refs and g in (8, 9):
            c = j % 3
            win_refs[g - 8][0, 2 * c:2 * c + 2] = blk.T.reshape(2, HEAD_DIM, tm)


def _inproj(x, w, cr, sr, cd, sd, tm, seq_len=None):
    m = x.shape[0]
    tab = pl.BlockSpec((tm, LANES), lambda i: (i, 0))
    out_specs = [pl.BlockSpec((N_BLOCKS_IN, tm, LANES), lambda i: (0, i, 0))]
    out_shape = [jax.ShapeDtypeStruct((N_BLOCKS_IN, m, LANES), f32)]
    if seq_len is not None:
        nt = seq_len // tm
        win = pl.BlockSpec((1, H_DIL, HEAD_DIM, tm), lambda i: (i // nt, 0, 0, i % nt))
        out_specs += [win, win]
        out_shape += [jax.ShapeDtypeStruct((m // seq_len, H_DIL, HEAD_DIM, seq_len), f32)] * 2
    return pl.pallas_call(
        _inproj_kernel,
        grid=(m // tm,),
        in_specs=[pl.BlockSpec((tm, D_MODEL), lambda i: (i, 0)),
                  pl.BlockSpec((D_MODEL, N_BLOCKS_IN * LANES), lambda i: (0, 0)),
                  tab, tab, tab, tab],
        out_specs=out_specs,
        out_shape=out_shape,
        compiler_params=_cparams(("parallel",)),
        name="inproj",
    )(x, w, cr, sr, cd, sd)


def _softplus(z):
    return jnp.maximum(z, 0.0) + jnp.log(1.0 + jnp.exp(-jnp.abs(z)))


def _rwkv_prep_core(x, x_prev, rkv, rkv_prev,
                    mu_rkv_ref, mu_wag_ref, vec_ref, w1_ref, w2_ref, a1_ref, a2_ref, g1_ref, g2_ref, seg_ref,
                    wo_ref, ko_ref, vo_ref, kko_ref, bo_ref, uo_ref, kro_ref, go_ref, bon_ref):
    xx = x_prev - x
    xw = (x + xx * mu_wag_ref[0:1, :]).astype(bf16)
    xa = (x + xx * mu_wag_ref[1:2, :]).astype(bf16)
    xg = (x + xx * mu_wag_ref[2:3, :]).astype(bf16)
    w0, a0, k_k, k_a, r_k = (vec_ref[i:i + 1, :] for i in range(5))

    lw = jnp.tanh(jnp.dot(xw, w1_ref[...], preferred_element_type=f32))
    wl = w0 + jnp.dot(lw.astype(bf16), w2_ref[...], preferred_element_type=f32)
    w_log = -_softplus(-wl) - 0.5
    decay = jnp.exp(-jnp.exp(w_log))
    la = jnp.dot(xa, a1_ref[...], preferred_element_type=f32)
    a = jax.nn.sigmoid(a0 + jnp.dot(la.astype(bf16), a2_ref[...], preferred_element_type=f32))
    lg = jax.nn.sigmoid(jnp.dot(xg, g1_ref[...], preferred_element_type=f32))
    g = jnp.dot(lg.astype(bf16), g2_ref[...], preferred_element_type=f32)

    def shifted(i):
        cur = rkv[:, GROUP * i:GROUP * (i + 1)]
        return cur + mu_rkv_ref[i:i + 1, :] * (rkv_prev[:, GROUP * i:GROUP * (i + 1)] - cur)

    r = shifted(0)
    k = shifted(1)
    v = shifted(2)

    seg = seg_ref[...]
    kk = k * k_k
    ss = _seg_dot(kk * kk, seg)
    kk = kk * lax.rsqrt(jnp.maximum(ss, 1e-24))
    k = k * (1.0 + (a - 1.0) * k_a)
    bonus = _seg_dot(r * k * r_k, seg) * v
    b = kk * a
    b_dot_r = _seg_dot(b * r, seg)
    k_dot_r = _seg_dot(k * r, seg)

    _put_blocks(wo_ref, decay)
    _put_blocks(ko_ref, k)
    _put_blocks(vo_ref, v)
    _put_blocks(kko_ref, kk)
    _put_blocks(bo_ref, b)
    _put_blocks(uo_ref, decay * r - kk * b_dot_r)
    _put_blocks(kro_ref, k_dot_r)
    _put_blocks(go_ref, g)
    _put_blocks(bon_ref, bonus)


def _rwkv_prep_seq_kernel(x_ref, r_ref, k_ref, v_ref, *rest):
    params, outs, (x_carry, rkv_carry) = rest[:10], rest[10:19], rest[19:]
    ti = pl.program_id(1)

    @pl.when(ti == 0)
    def _():
        x_carry[...] = jnp.zeros(x_carry.shape, f32)
        rkv_carry[...] = jnp.zeros(rkv_carry.shape, f32)

    x = x_ref[...]
    rkv = jnp.concatenate([_cat_blocks(r_ref), _cat_blocks(k_ref), _cat_blocks(v_ref)], axis=1)
    tm = x.shape[0]

    def prev_rows(cur, carry):
        first = lax.broadcasted_iota(jnp.int32, cur.shape, 0) == 0
        prev = jnp.where(first, carry[7:8, :], pltpu.roll(cur, 1, 0))
        carry[...] = cur[tm - 8:tm]
        return prev

    _rwkv_prep_core(x, prev_rows(x, x_carry), rkv, prev_rows(rkv, rkv_carry), *params, *outs)


def _rwkv_prep_rows_kernel(x_ref, xp_ref, r_ref, k_ref, v_ref, rp_ref, kp_ref, vp_ref, *rest):
    rkv = jnp.concatenate([_cat_blocks(r_ref), _cat_blocks(k_ref), _cat_blocks(v_ref)], axis=1)
    rkv_prev = jnp.concatenate([_cat_blocks(rp_ref), _cat_blocks(kp_ref), _cat_blocks(vp_ref)], axis=1)
    _rwkv_prep_core(x_ref[...], xp_ref[...], rkv, rkv_prev, *rest)


def _rwkv_prep_prompt(x, p, params, n_seq, t, tm):
    nt = t // tm

    def grp(gi):
        return pl.BlockSpec((3, tm, LANES), lambda b, ti: (gi, b * nt + ti, 0))

    def full(a):
        return pl.BlockSpec(a.shape, lambda b, ti: (0,) * a.ndim)

    scan_out = jax.ShapeDtypeStruct((3, t, n_seq * LANES), f32)
    tok_out = jax.ShapeDtypeStruct((3, n_seq * t, LANES), f32)
    return pl.pallas_call(
        _rwkv_prep_seq_kernel,
        grid=(n_seq, nt),
        in_specs=[pl.BlockSpec((tm, D_MODEL), lambda b, ti: (b * nt + ti, 0)), grp(0), grp(1), grp(2)]
        + [full(a) for a in params],
        out_specs=[pl.BlockSpec((3, tm, LANES), lambda b, ti: (0, ti, b))] * 7
        + [pl.BlockSpec((3, tm, LANES), lambda b, ti: (0, b * nt + ti, 0))] * 2,
        out_shape=[scan_out] * 7 + [tok_out] * 2,
        scratch_shapes=[pltpu.VMEM((8, D_MODEL), f32), pltpu.VMEM((8, 3 * GROUP), f32)],
        compiler_params=_cparams(("parallel", "arbitrary")),
        name="rwkv_prep_prompt",
    )(x, p, p, p, *params)


def _rwkv_prep_rows(x, xp, p, pp, params, row0, n_rows, tm):
    i0 = row0 // tm

    def grp(gi):
        return pl.BlockSpec((3, tm, LANES), lambda i: (gi, i0 + i, 0))

    def grp_prev(gi):
        return pl.BlockSpec((3, tm, LANES), lambda i: (gi, i, 0))

    def full(a):
        return pl.BlockSpec(a.shape, lambda i: (0,) * a.ndim)

    out = jax.ShapeDtypeStruct((3, n_rows, LANES), f32)
    return pl.pallas_call(
        _rwkv_prep_rows_kernel,
        grid=(n_rows // tm,),
        in_specs=[pl.BlockSpec((tm, D_MODEL), lambda i: (i0 + i, 0)), pl.BlockSpec((tm, D_MODEL), lambda i: (i, 0)),
                  grp(0), grp(1), grp(2), grp_prev(0), grp_prev(1), grp_prev(2)] + [full(a) for a in params],
        out_specs=[pl.BlockSpec((3, tm, LANES), lambda i: (0, i, 0))] * 9,
        out_shape=[out] * 9,
        compiler_params=_cparams(("parallel",)),
        name="rwkv_prep_rows",
    )(x, xp, p, p, p, pp, pp, pp, *params)


def _rwkv_scan_kernel(w_ref, k_ref, v_ref, kk_ref, b_ref, u_ref, kr_ref, s0_ref, seg_ref, y_ref, so_ref, s_scr,
                      *, n_steps):
    ti = pl.program_id(1)
    bb = w_ref.shape[2] // LANES
    rows = 3 * bb * HEAD_DIM

    @pl.when(ti == 0)
    def _():
        s_scr[...] = s0_ref[...].reshape(rows, LANES)

    if n_steps < y_ref.shape[1]:
        y_ref[...] = jnp.zeros(y_ref.shape, f32)
    eye = ((lax.broadcasted_iota(jnp.int32, (rows, LANES), 0) % HEAD_DIM)
           == (lax.broadcasted_iota(jnp.int32, (rows, LANES), 1) % HEAD_DIM))
    seg = seg_ref[...]

    def rows_of(ref, t):
        tiles = []
        for c in range(3):
            row = ref[c, pl.ds(t, 1), :]
            tiles += [jnp.broadcast_to(row[:, LANES * b:LANES * (b + 1)], (HEAD_DIM, LANES)) for b in range(bb)]
        return jnp.concatenate(tiles, axis=0)

    def step(t, carry):
        s = s_scr[...]
        p = (s * rows_of(kk_ref, t)).astype(bf16)
        q = (s * rows_of(u_ref, t)).astype(bf16)
        vm = jnp.where(eye, rows_of(v_ref, t), 0.0)
        vhi = vm.astype(bf16)
        vlo = (vm - vhi.astype(f32)).astype(bf16)
        red1 = jnp.dot(jnp.concatenate([p, q], axis=0), seg, preferred_element_type=f32)
        red2 = jnp.dot(jnp.concatenate([vhi, vlo], axis=0), seg, preferred_element_type=f32)
        sa = red1[0:rows]
        vcol = red2[0:rows] + red2[rows:2 * rows]
        ycol = red1[rows:2 * rows] + vcol * rows_of(kr_ref, t)
        s_scr[...] = s * rows_of(w_ref, t) - sa * rows_of(b_ref, t) + vcol * rows_of(k_ref, t)
        ym = jnp.where(eye, ycol, 0.0)
        for c in range(3):
            y_ref[c, pl.ds(t, 1), :] = jnp.concatenate(
                [jnp.sum(ym[(c * bb + b) * HEAD_DIM:(c * bb + b + 1) * HEAD_DIM], axis=0, keepdims=True)
                 for b in range(bb)], axis=1)
        return carry

    lax.fori_loop(0, n_steps, step, 0)

    @pl.when(ti == pl.num_programs(1) - 1)
    def _():
        so_ref[...] = s_scr[...].reshape(so_ref.shape)


def _rwkv_scan(ops, s0, seg, bb, tc, n_steps):
    _, t, lanes = ops[0].shape
    b = lanes // LANES
    blk = pl.BlockSpec((3, tc, bb * LANES), lambda bi, ti: (0, ti, bi))
    st = pl.BlockSpec((3, bb, HEAD_DIM, LANES), lambda bi, ti: (0, bi, 0, 0))
    return pl.pallas_call(
        functools.partial(_rwkv_scan_kernel, n_steps=n_steps),
        grid=(b // bb, t // tc),
        in_specs=[blk] * 7 + [st, pl.BlockSpec((LANES, LANES), lambda bi, ti: (0, 0))],
        out_specs=[blk, st],
        out_shape=[jax.ShapeDtypeStruct((3, t, b * LANES), f32),
                   jax.ShapeDtypeStruct((3, b, HEAD_DIM, LANES), f32)],
        scratch_shapes=[pltpu.VMEM((3 * bb * HEAD_DIM, LANES), f32)],
        compiler_params=_cparams(("parallel", "arbitrary")),
        name="rwkv_scan",
    )(*ops, s0, seg)


def _log_gamma(h):
    return float(np.log(np.float32(1.0) - np.float32(2.0) ** np.float32(-5.0 - min(h, H_RET - 1))))


def _seg_dot(x, seg16):
    hi = x.astype(bf16)
    lo = (x - hi.astype(f32)).astype(bf16)
    return (jnp.dot(hi, seg16, preferred_element_type=f32) + jnp.dot(lo, seg16, preferred_element_type=f32))


def _retention_kernel(q_ref, k_ref, v_ref, g_ref, s0_ref, seg_ref, y_ref, so_ref, s_scr, *, rows, n_valid):
    ci = pl.program_id(1)
    n_seq = q_ref.shape[1] // rows

    @pl.when(ci == 0)
    def _():
        s_scr[...] = s0_ref[...]

    lo = _lane_lo((rows, LANES))
    ii = lax.broadcasted_iota(jnp.int32, (rows, rows), 0)
    jj = lax.broadcasted_iota(jnp.int32, (rows, rows), 1)
    rel = (ii - jj).astype(f32)
    causal = ii >= jj
    pos = lax.broadcasted_iota(jnp.int32, (rows, LANES), 0)
    posf = pos.astype(f32)
    row_ok = pos < n_valid
    sq_r = lax.broadcasted_iota(jnp.int32, (LANES, LANES), 0) < HEAD_DIM
    sq_c = lax.broadcasted_iota(jnp.int32, (LANES, LANES), 1) < HEAD_DIM
    same_head = sq_r == sq_c
    seg = seg_ref[...]
    items = [(sq, c) for sq in range(n_seq) for c in range(3)]
    decays = {}
    for c in range(3):
        lg0, lg1 = _log_gamma(2 * c), _log_gamma(2 * c + 1)
        lg_lane = jnp.where(lo, lg0, lg1)
        decays[c] = dict(
            dmat=[jnp.where(causal, jnp.exp(lg * jnp.maximum(rel, 0.0)), 0.0) for lg in (lg0, lg1)],
            cross=jnp.exp(lg_lane * (posf + 1.0)),
            k_in=jnp.exp(lg_lane * (float(n_valid) - 1.0 - posf)),
            state=jnp.exp(jnp.where(sq_c, lg0, lg1) * float(n_valid)))

    st1 = {}
    for sq, c in items:
        sl = slice(sq * rows, (sq + 1) * rows)
        q = q_ref[c, sl, :]
        k = jnp.where(row_ok, k_ref[c, sl, :], 0.0)
        k16 = k.astype(bf16)
        v16 = v_ref[c, sl, :].astype(bf16)
        scores = [lax.dot_general(jnp.where(lo if half == 0 else jnp.logical_not(lo), q, 0.0).astype(bf16), k16, _NT,
                                  preferred_element_type=f32) for half in range(2)]
        state = s_scr[sq, c]
        cross = jnp.dot(q.astype(bf16), state.astype(bf16), preferred_element_type=f32)
        upd = lax.dot_general((k * decays[c]["k_in"]).astype(bf16), v16, _TN, preferred_element_type=f32)
        s_scr[sq, c] = state * decays[c]["state"] + jnp.where(same_head, upd, 0.0)
        st1[sq, c] = (scores, cross, v16)

    outs = {}
    for sq, c in items:
        scores, cross, v16 = st1[sq, c]
        o_halves = [jnp.dot((scores[half] * decays[c]["dmat"][half]).astype(bf16), v16, preferred_element_type=f32)
                    for half in range(2)]
        outs[sq, c] = jnp.where(lo, o_halves[0], o_halves[1]) + cross * decays[c]["cross"]

    devs = {it: outs[it] - _seg_dot(outs[it], seg) for it in items}
    for sq, c in items:
        d = devs[sq, c]
        yn = d * lax.rsqrt(_seg_dot(d * d, seg) + RET_GN_EPS)
        g = g_ref[c, sq * rows:(sq + 1) * rows, :]
        y_ref[c, sq * rows:(sq + 1) * rows, :] = g * jax.nn.sigmoid(g) * yn

    @pl.when(ci == pl.num_programs(1) - 1)
    def _():
        so_ref[...] = s_scr[...]


def _retention(p, s0_pairs, seg16, n_seq, seq_blk, rows, n_chunks, n_valid, row0):
    assert seq_blk == 1 or n_chunks == 1
    blk_rows = seq_blk * rows

    def grp(gi):
        return pl.BlockSpec((3, blk_rows, LANES), lambda b, c: (gi, row0 // blk_rows + b * n_chunks + c, 0))

    st = pl.BlockSpec((seq_blk, 3, LANES, LANES), lambda b, c: (b, 0, 0, 0))
    return pl.pallas_call(
        functools.partial(_retention_kernel, rows=rows, n_valid=n_valid),
        grid=(n_seq // seq_blk, n_chunks),
        in_specs=[grp(3), grp(4), grp(5), grp(6), st, pl.BlockSpec(seg16.shape, lambda b, c: (0, 0))],
        out_specs=[pl.BlockSpec((3, blk_rows, LANES), lambda b, c: (0, b * n_chunks + c, 0)), st],
        out_shape=[jax.ShapeDtypeStruct((3, n_seq * n_chunks * rows, LANES), f32),
                   jax.ShapeDtypeStruct((n_seq, 3, LANES, LANES), f32)],
        scratch_shapes=[pltpu.VMEM((seq_blk, 3, LANES, LANES), f32)],
        compiler_params=_cparams(("parallel", "arbitrary")),
        name="retention",
    )(p, p, p, p, s0_pairs, seg16)


def _dil_prompt_kernel(q_ref, k_ref, v_ref, y_ref, o_scr, l_scr):
    i = pl.program_id(1)
    n_steps = pl.num_programs(1)
    blk = DIL_BLK
    lo = _lane_lo((blk, LANES))
    ii = lax.broadcasted_iota(jnp.int32, (blk, 2 * blk), 0)
    jj = lax.broadcasted_iota(jnp.int32, (blk, 2 * blk), 1)
    band = jnp.logical_and(jj >= ii, jj <= ii + blk)

    for p, dil in enumerate(DILATIONS):
        nb = q_ref.shape[1] // (dil * blk)
        r = i // nb
        cb = i % nb
        start_q = r + dil * blk * cb
        start_p = r + dil * blk * jnp.maximum(cb - 1, 0)
        valid = jnp.logical_and(band, jnp.logical_or(cb > 0, jj >= blk))

        def rows(ref, c, start):
            if dil == 1:
                return ref[c, pl.ds(start, blk), :]
            return ref[c, pl.ds(start, blk, stride=dil), :]

        for c in range(3):
            q = rows(q_ref, c, start_q)
            kwin = jnp.concatenate([rows(k_ref, c, start_p), rows(k_ref, c, start_q)], axis=0).astype(bf16)
            vwin = jnp.concatenate([rows(v_ref, c, start_p), rows(v_ref, c, start_q)], axis=0).astype(bf16)
            outs, lses = [], []
            for half in range(2):
                qm = jnp.where(lo if half == 0 else jnp.logical_not(lo), q, 0.0).astype(bf16)
                s = lax.dot_general(qm, kwin, _NT, preferred_element_type=f32)
                s = jnp.where(valid, s, -jnp.inf)
                mx = jnp.max(s, axis=-1, keepdims=True)
                e = jnp.exp(s - mx)
                tot = jnp.sum(e, axis=-1, keepdims=True)
                lse = mx + jnp.log(tot)
                prob = (e / tot).astype(bf16)
                outs.append(jnp.dot(prob, vwin, preferred_element_type=f32))
                lses.append(lse)
            o_pair = jnp.where(lo, outs[0], outs[1])
            l_pair = jnp.where(lo, lses[0], lses[1])
            if dil == 1:
                o_scr[p, c, pl.ds(start_q, blk), :] = o_pair
                l_scr[p, c, pl.ds(start_q, blk), :] = l_pair
            else:
                o_scr[p, c, pl.ds(start_q, blk, stride=dil), :] = o_pair
                l_scr[p, c, pl.ds(start_q, blk, stride=dil), :] = l_pair

    @pl.when(i == n_steps - 1)
    def _():
        def merge(j, carry):
            sl = pl.ds(pl.multiple_of(j * blk, blk), blk)
            for c in range(3):
                ls = [l_scr[p, c, sl, :] for p in range(3)]
                mx = jnp.maximum(jnp.maximum(ls[0], ls[1]), ls[2])
                es = [jnp.exp(l - mx) for l in ls]
                tot = es[0] + es[1] + es[2]
                acc = (es[0] / tot) * o_scr[0, c, sl, :]
                acc = acc + (es[1] / tot) * o_scr[1, c, sl, :]
                acc = acc + (es[2] / tot) * o_scr[2, c, sl, :]
                y_ref[c, sl, :] = acc
            return carry

        lax.fori_loop(0, q_ref.shape[1] // blk, merge, 0)


def _dil_prompt(p, n_seq, t):
    def grp(gi):
        return pl.BlockSpec((3, t, LANES), lambda b, i: (gi, b, 0))

    n_steps = t // DIL_BLK
    return pl.pallas_call(
        _dil_prompt_kernel,
        grid=(n_seq, n_steps),
        in_specs=[grp(7), grp(8), grp(9)],
        out_specs=pl.BlockSpec((3, t, LANES), lambda b, i: (0, b, 0)),
        out_shape=jax.ShapeDtypeStruct((3, n_seq * t, LANES), f32),
        scratch_shapes=[pltpu.VMEM((3, 3, t, LANES), f32), pltpu.VMEM((3, 3, t, LANES), f32)],
        compiler_params=_cparams(("parallel", "arbitrary")),
        name="dil_prompt",
    )(p, p, p)


N_NEW = 4


def _dil_sample_kernel(q_ref, kn_ref, vn_ref, kc_ref, vc_ref, *rest, n_buf, aliased):
    y_ref, ko_ref, vo_ref = rest[2:] if aliased else rest
    t_all = n_buf + LANES
    zpad = jnp.zeros((LANES - DEC_PAD, LANES), f32)
    qi = lax.broadcasted_iota(jnp.int32, (DEC_PAD, t_all), 0)
    tok = lax.broadcasted_iota(jnp.int32, (DEC_PAD, t_all), 1)
    dist = n_buf + qi - tok
    masks = [jnp.logical_and(jnp.logical_and(dist >= 0, dist <= DIL_BLK * dil), (dist & (dil - 1)) == 0)
             for dil in DILATIONS]
    row_ok = lax.broadcasted_iota(jnp.int32, (DEC_PAD, LANES), 0) < N_NEW
    tail_ok = lax.broadcasted_iota(jnp.int32, (HEAD_DIM, LANES), 1) < N_NEW

    for c in range(3):
        kn_t = jnp.concatenate([kn_ref[c], zpad], axis=0).T
        vn_t = jnp.concatenate([vn_ref[c], zpad], axis=0).T
        q_pair = q_ref[c]
        y_halves = []
        for half in range(2):
            h = 2 * c + half
            k_tail = jnp.where(tail_ok, kn_t[HEAD_DIM * half:HEAD_DIM * (half + 1)], 0.0)
            v_tail = jnp.where(tail_ok, vn_t[HEAD_DIM * half:HEAD_DIM * (half + 1)], 0.0)
            k_all = jnp.concatenate([kc_ref[0, 0, h], k_tail], axis=1)
            v_all = jnp.concatenate([vc_ref[0, 0, h], v_tail], axis=1)
            ko_ref[0, 0, h] = pltpu.roll(k_all, t_all - N_NEW, 1)[:, :n_buf]
            vo_ref[0, 0, h] = pltpu.roll(v_all, t_all - N_NEW, 1)[:, :n_buf]
            q16 = q_pair[:, HEAD_DIM * half:HEAD_DIM * (half + 1)].astype(bf16)
            s = jnp.dot(q16, k_all.astype(bf16), preferred_element_type=f32)
            probs, lses = [], []
            for msk in masks:
                sm = jnp.where(msk, s, -jnp.inf)
                mx = jnp.max(sm, axis=-1, keepdims=True)
                lse = mx + jnp.log(jnp.sum(jnp.exp(sm - mx), axis=-1, keepdims=True))
                probs.append(jnp.exp(sm - lse).astype(bf16))
                lses.append(lse)
            o = lax.dot_general(jnp.concatenate(probs, axis=0), v_all.astype(bf16), _NT,
                                preferred_element_type=f32)
            mx = jnp.maximum(jnp.maximum(lses[0], lses[1]), lses[2])
            es = [jnp.exp(l - mx) for l in lses]
            tot = es[0] + es[1] + es[2]
            y_halves.append((es[0] / tot) * o[0:DEC_PAD] + (es[1] / tot) * o[DEC_PAD:2 * DEC_PAD]
                            + (es[2] / tot) * o[2 * DEC_PAD:3 * DEC_PAD])
        y_ref[c] = jnp.where(row_ok, jnp.concatenate(y_halves, axis=1), 0.0)


def _dil_sample(p, cache_k_t, cache_v_t, layer, prev_outs, n_seq, row0):
    n_buf = cache_k_t.shape[-1]

    def grp(gi):
        return pl.BlockSpec((3, DEC_PAD, LANES), lambda b: (gi, row0 // DEC_PAD + b, 0))

    buf = pl.BlockSpec((1, 1, H_DIL, HEAD_DIM, n_buf), lambda b: (layer, b, 0, 0, 0))
    aliased = prev_outs is not None
    extra = list(prev_outs) if aliased else []
    return pl.pallas_call(
        functools.partial(_dil_sample_kernel, n_buf=n_buf, aliased=aliased),
        grid=(n_seq,),
        in_specs=[grp(7), grp(8), grp(9), buf, buf] + [pl.BlockSpec(memory_space=pl.ANY)] * len(extra),
        out_specs=[pl.BlockSpec((3, DEC_PAD, LANES), lambda b: (0, b, 0)), buf, buf],
        out_shape=[jax.ShapeDtypeStruct((3, n_seq * DEC_PAD, LANES), f32),
                   jax.ShapeDtypeStruct(cache_k_t.shape, f32),
                   jax.ShapeDtypeStruct(cache_v_t.shape, f32)],
        input_output_aliases={5: 1, 6: 2} if aliased else {},
        compiler_params=_cparams(("arbitrary",)),
        name="dil_sample",
    )(p, p, p, cache_k_t, cache_v_t, *extra)


def _layer_norm(z, gamma, beta):
    mu = jnp.mean(z, axis=-1, keepdims=True)
    d = z - mu
    var = jnp.mean(d * d, axis=-1, keepdims=True)
    return d * lax.rsqrt(var + LN_EPS) * gamma + beta


def _merge_kernel(ya_ref, bon_ref, g_ref, yb_ref, yc_ref, x_ref, lnx_ref, seg_ref, wo_ref, ln_ref, o_ref):
    yr = _cat_blocks(ya_ref)
    seg = seg_ref[...]
    mean = _seg_dot(yr, seg)
    d = yr - mean
    var = _seg_dot(d * d, seg)
    ya = d * lax.rsqrt(var + RWKV_GN_EPS) * lnx_ref[0:1, :] + lnx_ref[1:2, :]
    ya = (ya + _cat_blocks(bon_ref)) * _cat_blocks(g_ref)
    ycat = jnp.concatenate([ya, _cat_blocks(yb_ref), _cat_blocks(yc_ref)], axis=1).astype(bf16)
    h = jnp.dot(ycat, wo_ref[...], preferred_element_type=f32)
    o_ref[...] = _layer_norm(DEEPNORM_ALPHA * x_ref[...] + h, ln_ref[0:1, :], ln_ref[1:2, :])


def _merge_prompt(ya_tm, bonus, g, yb, yc, x, params, n_seq, t, tm):
    nt = t // tm
    tok = pl.BlockSpec((3, tm, LANES), lambda b, ti: (0, b * nt + ti, 0))
    rows = pl.BlockSpec((tm, D_MODEL), lambda b, ti: (b * nt + ti, 0))

    def full(a):
        return pl.BlockSpec(a.shape, lambda b, ti: (0,) * a.ndim)

    return pl.pallas_call(
        _merge_kernel,
        grid=(n_seq, nt),
        in_specs=[pl.BlockSpec((3, tm, LANES), lambda b, ti: (0, ti, b)), tok, tok, tok, tok, rows]
        + [full(a) for a in params],
        out_specs=rows,
        out_shape=jax.ShapeDtypeStruct(x.shape, f32),
        compiler_params=_cparams(("parallel", "parallel")),
        name="mixer_merge_prompt",
    )(ya_tm, bonus, g, yb, yc, x, *params)


def _merge_rows(ya, bonus, g, yb, yc, x, params, tm):
    blk = pl.BlockSpec((3, tm, LANES), lambda i: (0, i, 0))
    rows = pl.BlockSpec((tm, D_MODEL), lambda i: (i, 0))

    def full(a):
        return pl.BlockSpec(a.shape, lambda i: (0,) * a.ndim)

    return pl.pallas_call(
        _merge_kernel,
        grid=(x.shape[0] // tm,),
        in_specs=[blk] * 5 + [rows] + [full(a) for a in params],
        out_specs=rows,
        out_shape=jax.ShapeDtypeStruct(x.shape, f32),
        compiler_params=_cparams(("parallel",)),
        name="mixer_merge_rows",
    )(ya, bonus, g, yb, yc, x, *params)


EXPERT_BLOCK = 8


def _first_max(x, idx, axes, sentinel):
    mx = x
    for ax in axes:
        mx = jnp.max(mx, axis=ax, keepdims=True)
    am = jnp.where(x == mx, idx, sentinel)
    for ax in axes:
        am = jnp.min(am, axis=ax, keepdims=True)
    return mx, am


def _route(logits_t, bias_col):
    n_tok = logits_t.shape[1]
    per = N_EXPERTS // N_ROUTE_GROUPS
    scores = jax.nn.sigmoid(logits_t)
    shp = (N_ROUTE_GROUPS, per, n_tok)
    s3 = scores.reshape(shp)
    b3 = (scores + bias_col).reshape(shp)
    sub = lax.broadcasted_iota(jnp.int32, shp, 1)
    grp = lax.broadcasted_iota(jnp.int32, shp, 0)
    eid = grp * per + sub
    neg = -jnp.inf

    m1, i1 = _first_max(b3, sub, (1,), per)
    m2, _ = _first_max(jnp.where(sub == i1, neg, b3), sub, (1,), per)
    gscore = m1 + m2
    gid = lax.broadcasted_iota(jnp.int32, gscore.shape, 0)
    gsel = jnp.zeros(gscore.shape, jnp.bool_)
    for _ in range(TOPK_GROUPS):
        _, gi = _first_max(gscore, gid, (0,), N_ROUTE_GROUPS)
        hit = gid == gi
        gsel = jnp.logical_or(gsel, hit)
        gscore = jnp.where(hit, neg, gscore)
    masked = jnp.where(gsel, b3, neg)
    esel = jnp.zeros(shp, jnp.bool_)
    for _ in range(TOP_K):
        _, ei = _first_max(masked, eid, (1, 0), N_EXPERTS)
        hit = eid == ei
        esel = jnp.logical_or(esel, hit)
        masked = jnp.where(hit, neg, masked)
    top_s = jnp.where(esel, s3, 0.0)
    denom = jnp.sum(jnp.sum(top_s, axis=1, keepdims=True), axis=0, keepdims=True)
    return (top_s / denom * ROUTE_SCALE).reshape(N_EXPERTS, n_tok)


def _moe_kernel(x_ref, wr_ref, rb_ref, wg_ref, wu_ref, wd_ref, wsg_ref, wsu_ref, wsd_ref, ln_ref,
                o_ref, x16_scr, comb_scr, acc_scr):
    j = pl.program_id(1)

    @pl.when(j == 0)
    def _():
        x16 = x_ref[...].astype(bf16)
        x16_scr[...] = x16
        logits_t = lax.dot_general(wr_ref[...], x16, _NT, preferred_element_type=f32)
        comb_t = _route(logits_t, rb_ref[...])
        comb_t = jnp.concatenate([comb_t, jnp.zeros_like(comb_t)], axis=0)
        comb = comb_t.T
        comb_scr[0] = comb
        for jj in range(1, N_EXPERTS // EXPERT_BLOCK):
            comb_scr[jj] = pltpu.roll(comb, LANES - EXPERT_BLOCK * jj, 1)
        sg = jnp.dot(x16, wsg_ref[...], preferred_element_type=f32)
        su = jnp.dot(x16, wsu_ref[...], preferred_element_type=f32)
        hs = (sg * jax.nn.sigmoid(sg) * su).astype(bf16)
        acc_scr[...] = jnp.dot(hs, wsd_ref[...], preferred_element_type=f32)

    x16 = x16_scr[...]
    gate = jnp.dot(x16, wg_ref[...], preferred_element_type=f32)
    up = jnp.dot(x16, wu_ref[...], preferred_element_type=f32)
    comb = comb_scr[j]
    tm = comb.shape[0]
    cexp = jnp.concatenate([jnp.broadcast_to(comb[:, e:e + 1], (tm, D_EXPERT)) for e in range(EXPERT_BLOCK)], axis=1)
    h = (gate * jax.nn.sigmoid(gate) * up * cexp).astype(bf16)
    acc_scr[...] += jnp.dot(h, wd_ref[...], preferred_element_type=f32)

    @pl.when(j == pl.num_programs(1) - 1)
    def _():
        o_ref[...] = _layer_norm(DEEPNORM_ALPHA * x_ref[...] + acc_scr[...], ln_ref[0:1, :], ln_ref[1:2, :])


def _moe(x, wr_t, rb_col, wg, wu, wd, wsg, wsu, wsd, ln, tm):
    m = x.shape[0]
    nb = EXPERT_BLOCK * D_EXPERT

    def full(a):
        return pl.BlockSpec(a.shape, lambda i, j: (0,) * a.ndim)

    return pl.pallas_call(
        _moe_kernel,
        grid=(m // tm, N_EXPERTS // EXPERT_BLOCK),
        in_specs=[pl.BlockSpec((tm, D_MODEL), lambda i, j: (i, 0)), full(wr_t), full(rb_col),
                  pl.BlockSpec((D_MODEL, nb), lambda i, j: (0, j)),
                  pl.BlockSpec((D_MODEL, nb), lambda i, j: (0, j)),
                  pl.BlockSpec((nb, D_MODEL), lambda i, j: (j, 0)),
                  full(wsg), full(wsu), full(wsd), full(ln)],
        out_specs=pl.BlockSpec((tm, D_MODEL), lambda i, j: (i, 0)),
        out_shape=jax.ShapeDtypeStruct((m, D_MODEL), f32),
        scratch_shapes=[pltpu.VMEM((tm, D_MODEL), bf16),
                        pltpu.VMEM((N_EXPERTS // EXPERT_BLOCK, tm, LANES), f32),
                        pltpu.VMEM((tm, D_MODEL), f32)],
        compiler_params=_cparams(("parallel", "arbitrary")),
        name="moe",
    )(x, wr_t, rb_col, wg, wu, wd, wsg, wsu, wsd, ln)


def _pad_heads(a, n_heads):
    return jnp.pad(a, [(0, 0)] * (a.ndim - 1) + [(0, GROUP - n_heads * HEAD_DIM)])


def _pad_w_in(w):
    widths = [C_RWKV] * 3 + [C_RET] * 4 + [C_DIL] * 3
    parts, off = [], 0
    for wd in widths:
        parts.append(jnp.pad(w[:, off:off + wd], ((0, 0), (0, GROUP - wd))))
        off += wd
    return jnp.concatenate(parts, axis=1).astype(bf16)


def _rope_tables(pos, inv_freq):
    ang = pos.astype(f32)[:, None] * inv_freq[None, :]
    cos, sin = jnp.cos(ang), jnp.sin(ang)
    return jnp.tile(jnp.concatenate([cos, cos], axis=1), (1, 2)), jnp.tile(jnp.concatenate([-sin, sin], axis=1), (1, 2))


def _segment_matrix(width, scale):
    idx = np.arange(width) // HEAD_DIM
    return jnp.asarray((idx[:, None] == idx[None, :]).astype(np.float32) * scale)


def _pairs_from_heads(s):
    b = s.shape[0]
    s6 = jnp.pad(s, ((0, 0), (0, 1), (0, 0), (0, 0))).reshape(b, 3, 2, HEAD_DIM, HEAD_DIM)
    z = jnp.zeros((b, 3, HEAD_DIM, HEAD_DIM), s.dtype)
    top = jnp.concatenate([s6[:, :, 0], z], axis=-1)
    bot = jnp.concatenate([z, s6[:, :, 1]], axis=-1)
    return jnp.concatenate([top, bot], axis=-2)


def _heads_from_pairs(sp):
    b = sp.shape[0]
    h0 = sp[:, :, :HEAD_DIM, :HEAD_DIM]
    h1 = sp[:, :, HEAD_DIM:, HEAD_DIM:]
    return jnp.stack([h0, h1], axis=2).reshape(b, 6, HEAD_DIM, HEAD_DIM)[:, :H_RET]


def _rwkv_pairs_from_heads(s):
    b = s.shape[0]
    s6 = jnp.pad(s, ((0, 0), (0, 1), (0, 0), (0, 0))).reshape(b, 3, 2, HEAD_DIM, HEAD_DIM)
    return jnp.transpose(s6, (1, 0, 3, 2, 4)).reshape(3, b, HEAD_DIM, LANES)


def _rwkv_heads_from_pairs(sp):
    b = sp.shape[1]
    s6 = jnp.transpose(sp.reshape(3, b, HEAD_DIM, 2, HEAD_DIM), (1, 0, 3, 2, 4))
    return s6.reshape(b, 6, HEAD_DIM, HEAD_DIM)[:, :H_RWKV]


def kernel(x_prompt, x_sample, state_rwkv, state_shift, state_ret, cache_k_win, cache_v_win, w_in, mu_rkv, mu_wag, w0, w1, w2, a0, a1, a2, g1, g2, k_k, k_a, r_k, lnx_g, lnx_b, w_out, ln1_g, ln1_b, w_router, router_bias, w_gate, w_up, w_down, ws_gate, ws_up, ws_down, ln2_g, ln2_b):
    bp, tp, _ = x_prompt.shape
    bs, ts, _ = x_sample.shape
    n_buf = cache_k_win.shape[2]
    depth = w_in.shape[0]
    assert ts == N_NEW and n_buf == DILATIONS[-1] * DIL_BLK and tp % (DILATIONS[-1] * DIL_BLK) == 0
    mp = bp * tp
    ms = bs * DEC_PAD
    tm = 256
    tm_moe = 512
    assert ms % tm_moe == 0 and bs % 8 == 0

    inv_ret = 1.0 / (10000.0 ** jnp.linspace(0.0, 1.0, HEAD_DIM // 2, dtype=f32))
    inv_dil = ROPE_THETA ** (-jnp.arange(0, HEAD_DIM, 2, dtype=f32) / HEAD_DIM)
    pos_p = jnp.tile(jnp.arange(tp), bp)
    pos_s = jnp.tile(PAST_LEN + jnp.arange(DEC_PAD), bs)
    tabs_p = _rope_tables(pos_p, inv_ret) + _rope_tables(pos_p, inv_dil)
    tabs_s = _rope_tables(pos_s, inv_ret) + _rope_tables(pos_s, inv_dil)
    zero_tab = jnp.zeros((bs, LANES), f32)

    seg_sum = _segment_matrix(GROUP, 1.0).astype(bf16)
    seg_mean = _segment_matrix(GROUP, 1.0 / HEAD_DIM).astype(bf16)
    seg_mean_pair = _segment_matrix(LANES, 1.0 / HEAD_DIM).astype(bf16)
    seg_pair_bf16 = _segment_matrix(LANES, 1.0).astype(bf16)

    x_p = x_prompt.reshape(mp, D_MODEL)
    x_s = jnp.pad(x_sample, ((0, 0), (0, DEC_PAD - ts), (0, 0))).reshape(ms, D_MODEL)

    cache_k_t = jnp.transpose(cache_k_win, (0, 1, 3, 4, 2))
    cache_v_t = jnp.transpose(cache_v_win, (0, 1, 3, 4, 2))
    win_outs = None

    outs_p = [[] for _ in range(5)]
    outs_s = [[] for _ in range(3)]
    for l in range(depth):
        xp3 = x_p.reshape(bp, tp, D_MODEL)
        xs3 = x_s.reshape(bs, DEC_PAD, D_MODEL)
        x_last_s = state_shift[l]
        outs_p[1].append(xp3[:, -1])
        outs_s[1].append(xs3[:, ts - 1])

        w_in_p = _pad_w_in(w_in[l])
        p_p, k_new_t, v_new_t = _inproj(x_p, w_in_p, *tabs_p, tm, seq_len=tp)
        p_s, = _inproj(x_s, w_in_p, *tabs_s, tm)
        p_last, = _inproj(x_last_s, w_in_p, zero_tab, zero_tab, zero_tab, zero_tab, min(bs, 128))

        xs_prev = jnp.concatenate([x_last_s[:, None], xs3[:, :-1]], axis=1).reshape(ms, D_MODEL)
        pr_s = p_s[:9].reshape(9, bs, DEC_PAD, LANES)
        pp_s = jnp.concatenate([p_last[:9][:, :, None], pr_s[:, :, :-1]], axis=2).reshape(9, ms, LANES)

        vec = jnp.stack([_pad_heads(v, H_RWKV) for v in (w0[l], a0[l], k_k[l], k_a[l], r_k[l].reshape(-1))])
        prep_params = (_pad_heads(mu_rkv[l], H_RWKV), mu_wag[l], vec,
                       w1[l].astype(bf16), _pad_heads(w2[l], H_RWKV).astype(bf16),
                       a1[l].astype(bf16), _pad_heads(a2[l], H_RWKV).astype(bf16),
                       g1[l].astype(bf16), _pad_heads(g2[l], H_RWKV).astype(bf16), seg_sum)
        prep_p = _rwkv_prep_prompt(x_p, p_p, prep_params, bp, tp, tm)
        prep_s = _rwkv_prep_rows(x_s, xs_prev, p_s, pp_s, prep_params, 0, ms, tm)

        ops_s = [jnp.swapaxes(a.reshape(3, bs, DEC_PAD, LANES), 1, 2).reshape(3, DEC_PAD, bs * LANES)
                 for a in prep_s[:7]]
        ya_p, s_rwkv_p = _rwkv_scan(list(prep_p[:7]), jnp.zeros((3, bp, HEAD_DIM, LANES), f32), seg_pair_bf16,
                                    bp, 64, 64)
        ya_s, s_rwkv_s = _rwkv_scan(ops_s, _rwkv_pairs_from_heads(state_rwkv[l]), seg_pair_bf16, 8, DEC_PAD, ts)
        ya_s = jnp.swapaxes(ya_s.reshape(3, DEC_PAD, bs, LANES), 1, 2).reshape(3, ms, LANES)
        outs_p[0].append(_rwkv_heads_from_pairs(s_rwkv_p))
        outs_s[0].append(_rwkv_heads_from_pairs(s_rwkv_s))

        yb_p, s_ret_p = _retention(p_p, jnp.zeros((bp, 3, LANES, LANES), f32), seg_mean_pair,
                                   bp, 1, RET_CHUNK, tp // RET_CHUNK, RET_CHUNK, 0)
        yb_s, s_ret_s = _retention(p_s, _pairs_from_heads(state_ret[l]), seg_mean_pair, bs, 8, DEC_PAD, 1, ts, 0)
        outs_p[2].append(_heads_from_pairs(s_ret_p))
        outs_s[2].append(_heads_from_pairs(s_ret_s))

        yc_p = _dil_prompt(p_p, bp, tp)
        yc_s, k_win_t, v_win_t = _dil_sample(p_s, cache_k_t, cache_v_t, l, win_outs, bs, 0)
        win_outs = (k_win_t, v_win_t)
        keep = min(n_buf, tp)
        outs_p[3].append(k_new_t[..., tp - keep:])
        outs_p[4].append(v_new_t[..., tp - keep:])

        w_out_p = jnp.concatenate([
            jnp.pad(w_out[l][:C_RWKV], ((0, GROUP - C_RWKV), (0, 0))),
            jnp.pad(w_out[l][C_RWKV:C_RWKV + C_RET], ((0, GROUP - C_RET), (0, 0))),
            w_out[l][C_RWKV + C_RET:]], axis=0).astype(bf16)
        lnx = jnp.stack([_pad_heads(lnx_g[l], H_RWKV), _pad_heads(lnx_b[l], H_RWKV)])
        merge_params = (lnx, seg_mean, w_out_p, jnp.stack([ln1_g[l], ln1_b[l]]))
        x_p = _merge_prompt(ya_p, prep_p[8], prep_p[7], yb_p, yc_p, x_p, merge_params, bp, tp, tm)
        x_s = _merge_rows(ya_s, prep_s[8], prep_s[7], yb_s, yc_s, x_s, merge_params, tm)

        wg = jnp.transpose(w_gate[l], (1, 0, 2)).reshape(D_MODEL, N_EXPERTS * D_EXPERT).astype(bf16)
        wu = jnp.transpose(w_up[l], (1, 0, 2)).reshape(D_MODEL, N_EXPERTS * D_EXPERT).astype(bf16)
        wd = w_down[l].reshape(N_EXPERTS * D_EXPERT, D_MODEL).astype(bf16)
        moe_params = (w_router[l].T.astype(bf16), router_bias[l].reshape(N_EXPERTS, 1), wg, wu, wd,
                      ws_gate[l].astype(bf16), ws_up[l].astype(bf16), ws_down[l].astype(bf16),
                      jnp.stack([ln2_g[l], ln2_b[l]]))
        x_p = _moe(x_p, *moe_params, tm_moe)
        x_s = _moe(x_s, *moe_params, tm_moe)

    y_prompt = x_p.reshape(bp, tp, D_MODEL)
    y_sample = x_s.reshape(bs, DEC_PAD, D_MODEL)[:, :ts]
    wins_p = tuple(jnp.transpose(jnp.stack(o), (0, 1, 4, 2, 3)) for o in outs_p[3:])
    wins_s = tuple(jnp.transpose(w, (0, 1, 4, 2, 3)) for w in win_outs)
    return ((y_prompt, y_sample) + tuple(jnp.stack(o) for o in outs_p[:3]) + wins_p
            + tuple(jnp.stack(o) for o in outs_s) + wins_s)
```

```python
import functools

import numpy as np
import jax
import jax.numpy as jnp
from jax import lax
from jax.experimental import pallas as pl
from jax.experimental.pallas import tpu as pltpu

f32 = jnp.float32
bf16 = jnp.bfloat16

D_MODEL = 1024
HEAD_DIM = 64
LANES = 128
SUBLANES = 8
H_RWKV = 5
H_RET = 5
H_DIL = 6
GROUP = 384
N_GROUPS_IN = 10
N_BLOCKS_IN = 3 * N_GROUPS_IN
C_RWKV = H_RWKV * HEAD_DIM
C_RET = H_RET * HEAD_DIM
C_DIL = H_DIL * HEAD_DIM
PAST_LEN = 2048
DEC_PAD = SUBLANES
RET_CHUNK = 128
RWKV_GN_EPS = 64e-5
RET_GN_EPS = 1e-5
DILATIONS = (1, 4, 16)
DIL_BLK = 128
ROPE_THETA = 10000.0
N_EXPERTS = 64
D_EXPERT = 128
TOP_K = 8
N_ROUTE_GROUPS = 8
TOPK_GROUPS = 4
ROUTE_SCALE = 2.5
LN_EPS = 1e-5
DEPTH = 2
DEEPNORM_ALPHA = (2 * DEPTH) ** 0.25
VMEM_LIMIT = 56 * 1024 * 1024

ROW_TILE = 256
MOE_ROW_TILE = 512
SCAN_CHUNK = 64
SEQ_BLOCK = 8

_NT = (((1,), (1,)), ((), ()))
_TN = (((0,), (0,)), ((), ()))


def _cparams(sem):
    return pltpu.CompilerParams(dimension_semantics=sem, vmem_limit_bytes=VMEM_LIMIT)


def _lane_lo(shape):
    return (lax.broadcasted_iota(jnp.int32, shape, len(shape) - 1) % LANES) < HEAD_DIM


def _cat_blocks(ref):
    return jnp.concatenate([ref[0], ref[1], ref[2]], axis=1)


def _put_blocks(ref, val):
    for c in range(3):
        ref[c] = val[:, LANES * c:LANES * (c + 1)]


def _inproj_kernel(x_ref, w_ref, cr_ref, sr_ref, cd_ref, sd_ref, p_ref, *win_refs):
    acc = jnp.dot(x_ref[...].astype(bf16), w_ref[...], preferred_element_type=f32)
    tm = acc.shape[0]
    first = (lax.broadcasted_iota(jnp.int32, (tm, LANES), 1) % HEAD_DIM) < HEAD_DIM // 2

    def rot(xb, c, s):
        partner = jnp.where(first, pltpu.roll(xb, LANES - HEAD_DIM // 2, 1), pltpu.roll(xb, HEAD_DIM // 2, 1))
        return xb * c + partner * s

    for j in range(N_BLOCKS_IN):
        blk = acc[:, LANES * j:LANES * (j + 1)]
        g = j // 3
        if g in (3, 4):
            blk = rot(blk, cr_ref[...], sr_ref[...])
        if g in (7, 8):
            blk = rot(blk, cd_ref[...], sd_ref[...])
        if g in (4, 7):
            blk = blk * (HEAD_DIM ** -0.5)
        p_ref[j] = blk
        if win_refs and g in (8, 9):
            c = j % 3
            win_refs[g - 8][0, 2 * c:2 * c + 2] = blk.T.reshape(2, HEAD_DIM, tm)


def _inproj(x, w, cr, sr, cd, sd, tm, seq_len=None):
    m = x.shape[0]
    tab = pl.BlockSpec((tm, LANES), lambda i: (i, 0))
    out_specs = [pl.BlockSpec((N_BLOCKS_IN, tm, LANES), lambda i: (0, i, 0))]
    out_shape = [jax.ShapeDtypeStruct((N_BLOCKS_IN, m, LANES), f32)]
    if seq_len is not None:
        nt = seq_len // tm
        win = pl.BlockSpec((1, H_DIL, HEAD_DIM, tm), lambda i: (i // nt, 0, 0, i % nt))
        out_specs += [win, win]
        out_shape += [jax.ShapeDtypeStruct((m // seq_len, H_DIL, HEAD_DIM, seq_len), f32)] * 2
    return pl.pallas_call(
        _inproj_kernel,
        grid=(m // tm,),
        in_specs=[pl.BlockSpec((tm, D_MODEL), lambda i: (i, 0)),
                  pl.BlockSpec((D_MODEL, N_BLOCKS_IN * LANES), lambda i: (0, 0)),
                  tab, tab, tab, tab],
        out_specs=out_specs,
        out_shape=out_shape,
        compiler_params=_cparams(("parallel",)),
        name="inproj",
    )(x, w, cr, sr, cd, sd)


def _softplus(z):
    return jnp.maximum(z, 0.0) + jnp.log(1.0 + jnp.exp(-jnp.abs(z)))


def _rwkv_prep_core(x, x_prev, rkv, rkv_prev,
                    mu_rkv_ref, mu_wag_ref, vec_ref, w1_ref, w2_ref, a1_ref, a2_ref, g1_ref, g2_ref, seg_ref,
                    wo_ref, ko_ref, vo_ref, kko_ref, bo_ref, uo_ref, kro_ref, go_ref, bon_ref):
    xx = x_prev - x
    xw = (x + xx * mu_wag_ref[0:1, :]).astype(bf16)
    xa = (x + xx * mu_wag_ref[1:2, :]).astype(bf16)
    xg = (x + xx * mu_wag_ref[2:3, :]).astype(bf16)
    w0, a0, k_k, k_a, r_k = (vec_ref[i:i + 1, :] for i in range(5))

    lw = jnp.tanh(jnp.dot(xw, w1_ref[...], preferred_element_type=f32))
    wl = w0 + jnp.dot(lw.astype(bf16), w2_ref[...], preferred_element_type=f32)
    w_log = -_softplus(-wl) - 0.5
    decay = jnp.exp(-jnp.exp(w_log))
    la = jnp.dot(xa, a1_ref[...], preferred_element_type=f32)
    a = jax.nn.sigmoid(a0 + jnp.dot(la.astype(bf16), a2_ref[...], preferred_element_type=f32))
    lg = jax.nn.sigmoid(jnp.dot(xg, g1_ref[...], preferred_element_type=f32))
    g = jnp.dot(lg.astype(bf16), g2_ref[...], preferred_element_type=f32)

    def shifted(i):
        cur = rkv[:, GROUP * i:GROUP * (i + 1)]
        return cur + mu_rkv_ref[i:i + 1, :] * (rkv_prev[:, GROUP * i:GROUP * (i + 1)] - cur)

    r = shifted(0)
    k = shifted(1)
    v = shifted(2)

    seg = seg_ref[...]
    kk = k * k_k
    ss = _seg_dot(kk * kk, seg)
    kk = kk * lax.rsqrt(jnp.maximum(ss, 1e-24))
    k = k * (1.0 + (a - 1.0) * k_a)
    bonus = _seg_dot(r * k * r_k, seg) * v
    b = kk * a
    b_dot_r = _seg_dot(b * r, seg)
    k_dot_r = _seg_dot(k * r, seg)

    _put_blocks(wo_ref, decay)
    _put_blocks(ko_ref, k)
    _put_blocks(vo_ref, v)
    _put_blocks(kko_ref, kk)
    _put_blocks(bo_ref, b)
    _put_blocks(uo_ref, decay * r - kk * b_dot_r)
    _put_blocks(kro_ref, k_dot_r)
    _put_blocks(go_ref, g)
    _put_blocks(bon_ref, bonus)


def _rwkv_prep_seq_kernel(x_ref, r_ref, k_ref, v_ref, *rest):
    params, outs, (x_carry, rkv_carry) = rest[:10], rest[10:19], rest[19:]
    ti = pl.program_id(1)

    @pl.when(ti == 0)
    def _():
        x_carry[...] = jnp.zeros(x_carry.shape, f32)
        rkv_carry[...] = jnp.zeros(rkv_carry.shape, f32)

    x = x_ref[...]
    rkv = jnp.concatenate([_cat_blocks(r_ref), _cat_blocks(k_ref), _cat_blocks(v_ref)], axis=1)
    tm = x.shape[0]

    def prev_rows(cur, carry):
        first = lax.broadcasted_iota(jnp.int32, cur.shape, 0) == 0
        prev = jnp.where(first, carry[SUBLANES - 1:SUBLANES, :], pltpu.roll(cur, 1, 0))
        carry[...] = cur[tm - SUBLANES:tm]
        return prev

    _rwkv_prep_core(x, prev_rows(x, x_carry), rkv, prev_rows(rkv, rkv_carry), *params, *outs)


def _rwkv_prep_rows_kernel(x_ref, xp_ref, r_ref, k_ref, v_ref, rp_ref, kp_ref, vp_ref, *rest):
    rkv = jnp.concatenate([_cat_blocks(r_ref), _cat_blocks(k_ref), _cat_blocks(v_ref)], axis=1)
    rkv_prev = jnp.concatenate([_cat_blocks(rp_ref), _cat_blocks(kp_ref), _cat_blocks(vp_ref)], axis=1)
    _rwkv_prep_core(x_ref[...], xp_ref[...], rkv, rkv_prev, *rest)


def _rwkv_prep_prompt(x, p, params, n_seq, t, tm):
    nt = t // tm

    def grp(gi):
        return pl.BlockSpec((3, tm, LANES), lambda b, ti: (gi, b * nt + ti, 0))

    def full(a):
        return pl.BlockSpec(a.shape, lambda b, ti: (0,) * a.ndim)

    scan_out = jax.ShapeDtypeStruct((3, t, n_seq * LANES), f32)
    tok_out = jax.ShapeDtypeStruct((3, n_seq * t, LANES), f32)
    return pl.pallas_call(
        _rwkv_prep_seq_kernel,
        grid=(n_seq, nt),
        in_specs=[pl.BlockSpec((tm, D_MODEL), lambda b, ti: (b * nt + ti, 0)), grp(0), grp(1), grp(2)]
        + [full(a) for a in params],
        out_specs=[pl.BlockSpec((3, tm, LANES), lambda b, ti: (0, ti, b))] * 7
        + [pl.BlockSpec((3, tm, LANES), lambda b, ti: (0, b * nt + ti, 0))] * 2,
        out_shape=[scan_out] * 7 + [tok_out] * 2,
        scratch_shapes=[pltpu.VMEM((SUBLANES, D_MODEL), f32), pltpu.VMEM((SUBLANES, 3 * GROUP), f32)],
        compiler_params=_cparams(("parallel", "arbitrary")),
        name="rwkv_prep_prompt",
    )(x, p, p, p, *params)


def _rwkv_prep_rows(x, xp, p, pp, params, row0, n_rows, tm):
    i0 = row0 // tm

    def grp(gi):
        return pl.BlockSpec((3, tm, LANES), lambda i: (gi, i0 + i, 0))

    def grp_prev(gi):
        return pl.BlockSpec((3, tm, LANES), lambda i: (gi, i, 0))

    def full(a):
        return pl.BlockSpec(a.shape, lambda i: (0,) * a.ndim)

    out = jax.ShapeDtypeStruct((3, n_rows, LANES), f32)
    return pl.pallas_call(
        _rwkv_prep_rows_kernel,
        grid=(n_rows // tm,),
        in_specs=[pl.BlockSpec((tm, D_MODEL), lambda i: (i0 + i, 0)), pl.BlockSpec((tm, D_MODEL), lambda i: (i, 0)),
                  grp(0), grp(1), grp(2), grp_prev(0), grp_prev(1), grp_prev(2)] + [full(a) for a in params],
        out_specs=[pl.BlockSpec((3, tm, LANES), lambda i: (0, i, 0))] * 9,
        out_shape=[out] * 9,
        compiler_params=_cparams(("parallel",)),
        name="rwkv_prep_rows",
    )(x, xp, p, p, p, pp, pp, pp, *params)


def _rwkv_scan_kernel(w_ref, k_ref, v_ref, kk_ref, b_ref, u_ref, kr_ref, s0_ref, seg_ref, y_ref, so_ref, s_scr,
                      *, n_steps):
    ti = pl.program_id(1)
    bb = w_ref.shape[2] // LANES
    rows = 3 * bb * HEAD_DIM

    @pl.when(ti == 0)
    def _():
        s_scr[...] = s0_ref[...].reshape(rows, LANES)

    if n_steps < y_ref.shape[1]:
        y_ref[...] = jnp.zeros(y_ref.shape, f32)
    eye = ((lax.broadcasted_iota(jnp.int32, (rows, LANES), 0) % HEAD_DIM)
           == (lax.broadcasted_iota(jnp.int32, (rows, LANES), 1) % HEAD_DIM))
    seg = seg_ref[...]

    def rows_of(ref, t):
        tiles = []
        for c in range(3):
            row = ref[c, pl.ds(t, 1), :]
            tiles += [jnp.broadcast_to(row[:, LANES * b:LANES * (b + 1)], (HEAD_DIM, LANES)) for b in range(bb)]
        return jnp.concatenate(tiles, axis=0)

    def step(t, carry):
        s = s_scr[...]
        p = (s * rows_of(kk_ref, t)).astype(bf16)
        q = (s * rows_of(u_ref, t)).astype(bf16)
        vm = jnp.where(eye, rows_of(v_ref, t), 0.0)
        vhi = vm.astype(bf16)
        vlo = (vm - vhi.astype(f32)).astype(bf16)
        red1 = jnp.dot(jnp.concatenate([p, q], axis=0), seg, preferred_element_type=f32)
        red2 = jnp.dot(jnp.concatenate([vhi, vlo], axis=0), seg, preferred_element_type=f32)
        sa = red1[0:rows]
        vcol = red2[0:rows] + red2[rows:2 * rows]
        ycol = red1[rows:2 * rows] + vcol * rows_of(kr_ref, t)
        s_scr[...] = s * rows_of(w_ref, t) - sa * rows_of(b_ref, t) + vcol * rows_of(k_ref, t)
        ym = jnp.where(eye, ycol, 0.0)
        for c in range(3):
            y_ref[c, pl.ds(t, 1), :] = jnp.concatenate(
                [jnp.sum(ym[(c * bb + b) * HEAD_DIM:(c * bb + b + 1) * HEAD_DIM], axis=0, keepdims=True)
                 for b in range(bb)], axis=1)
        return carry

    lax.fori_loop(0, n_steps, step, 0)

    @pl.when(ti == pl.num_programs(1) - 1)
    def _():
        so_ref[...] = s_scr[...].reshape(so_ref.shape)


def _rwkv_scan(ops, s0, seg, bb, tc, n_steps):
    _, t, lanes = ops[0].shape
    b = lanes // LANES
    blk = pl.BlockSpec((3, tc, bb * LANES), lambda bi, ti: (0, ti, bi))
    st = pl.BlockSpec((3, bb, HEAD_DIM, LANES), lambda bi, ti: (0, bi, 0, 0))
    return pl.pallas_call(
        functools.partial(_rwkv_scan_kernel, n_steps=n_steps),
        grid=(b // bb, t // tc),
        in_specs=[blk] * 7 + [st, pl.BlockSpec((LANES, LANES), lambda bi, ti: (0, 0))],
        out_specs=[blk, st],
        out_shape=[jax.ShapeDtypeStruct((3, t, b * LANES), f32),
                   jax.ShapeDtypeStruct((3, b, HEAD_DIM, LANES), f32)],
        scratch_shapes=[pltpu.VMEM((3 * bb * HEAD_DIM, LANES), f32)],
        compiler_params=_cparams(("parallel", "arbitrary")),
        name="rwkv_scan",
    )(*ops, s0, seg)


def _log_gamma(h):
    return float(np.log(np.float32(1.0) - np.float32(2.0) ** np.float32(-5.0 - min(h, H_RET - 1))))


def _seg_dot(x, seg16):
    hi = x.astype(bf16)
    lo = (x - hi.astype(f32)).astype(bf16)
    return (jnp.dot(hi, seg16, preferred_element_type=f32) + jnp.dot(lo, seg16, preferred_element_type=f32))


def _retention_kernel(q_ref, k_ref, v_ref, g_ref, s0_ref, seg_ref, y_ref, so_ref, s_scr, *, rows, n_valid):
    ci = pl.program_id(1)
    n_seq = q_ref.shape[1] // rows

    @pl.when(ci == 0)
    def _():
        s_scr[...] = s0_ref[...]

    lo = _lane_lo((rows, LANES))
    ii = lax.broadcasted_iota(jnp.int32, (rows, rows), 0)
    jj = lax.broadcasted_iota(jnp.int32, (rows, rows), 1)
    rel = (ii - jj).astype(f32)
    causal = ii >= jj
    pos = lax.broadcasted_iota(jnp.int32, (rows, LANES), 0)
    posf = pos.astype(f32)
    row_ok = pos < n_valid
    sq_r = lax.broadcasted_iota(jnp.int32, (LANES, LANES), 0) < HEAD_DIM
    sq_c = lax.broadcasted_iota(jnp.int32, (LANES, LANES), 1) < HEAD_DIM
    same_head = sq_r == sq_c
    seg = seg_ref[...]
    items = [(sq, c) for sq in range(n_seq) for c in range(3)]
    decays = {}
    for c in range(3):
        lg0, lg1 = _log_gamma(2 * c), _log_gamma(2 * c + 1)
        lg_lane = jnp.where(lo, lg0, lg1)
        decays[c] = dict(
            dmat=[jnp.where(causal, jnp.exp(lg * jnp.maximum(rel, 0.0)), 0.0) for lg in (lg0, lg1)],
            cross=jnp.exp(lg_lane * (posf + 1.0)),
            k_in=jnp.exp(lg_lane * (float(n_valid) - 1.0 - posf)),
            state=jnp.exp(jnp.where(sq_c, lg0, lg1) * float(n_valid)))

    st1 = {}
    for sq, c in items:
        sl = slice(sq * rows, (sq + 1) * rows)
        q = q_ref[c, sl, :]
        k = jnp.where(row_ok, k_ref[c, sl, :], 0.0)
        k16 = k.astype(bf16)
        v16 = v_ref[c, sl, :].astype(bf16)
        scores = [lax.dot_general(jnp.where(lo if half == 0 else jnp.logical_not(lo), q, 0.0).astype(bf16), k16, _NT,
                                  preferred_element_type=f32) for half in range(2)]
        state = s_scr[sq, c]
        cross = jnp.dot(q.astype(bf16), state.astype(bf16), preferred_element_type=f32)
        upd = lax.dot_general((k * decays[c]["k_in"]).astype(bf16), v16, _TN, preferred_element_type=f32)
        s_scr[sq, c] = state * decays[c]["state"] + jnp.where(same_head, upd, 0.0)
        st1[sq, c] = (scores, cross, v16)

    outs = {}
    for sq, c in items:
        scores, cross, v16 = st1[sq, c]
        o_halves = [jnp.dot((scores[half] * decays[c]["dmat"][half]).astype(bf16), v16, preferred_element_type=f32)
                    for half in range(2)]
        outs[sq, c] = jnp.where(lo, o_halves[0], o_halves[1]) + cross * decays[c]["cross"]

    devs = {it: outs[it] - _seg_dot(outs[it], seg) for it in items}
    for sq, c in items:
        d = devs[sq, c]
        yn = d * lax.rsqrt(_seg_dot(d * d, seg) + RET_GN_EPS)
        g = g_ref[c, sq * rows:(sq + 1) * rows, :]
        y_ref[c, sq * rows:(sq + 1) * rows, :] = g * jax.nn.sigmoid(g) * yn

    @pl.when(ci == pl.num_programs(1) - 1)
    def _():
        so_ref[...] = s_scr[...]


def _retention(p, s0_pairs, seg16, n_seq, seq_blk, rows, n_chunks, n_valid, row0):
    assert seq_blk == 1 or n_chunks == 1
    blk_rows = seq_blk * rows

    def grp(gi):
        return pl.BlockSpec((3, blk_rows, LANES), lambda b, c: (gi, row0 // blk_rows + b * n_chunks + c, 0))

    st = pl.BlockSpec((seq_blk, 3, LANES, LANES), lambda b, c: (b, 0, 0, 0))
    return pl.pallas_call(
        functools.partial(_retention_kernel, rows=rows, n_valid=n_valid),
        grid=(n_seq // seq_blk, n_chunks),
        in_specs=[grp(3), grp(4), grp(5), grp(6), st, pl.BlockSpec(seg16.shape, lambda b, c: (0, 0))],
        out_specs=[pl.BlockSpec((3, blk_rows, LANES), lambda b, c: (0, b * n_chunks + c, 0)), st],
        out_shape=[jax.ShapeDtypeStruct((3, n_seq * n_chunks * rows, LANES), f32),
                   jax.ShapeDtypeStruct((n_seq, 3, LANES, LANES), f32)],
        scratch_shapes=[pltpu.VMEM((seq_blk, 3, LANES, LANES), f32)],
        compiler_params=_cparams(("parallel", "arbitrary")),
        name="retention",
    )(p, p, p, p, s0_pairs, seg16)


def _dil_prompt_kernel(q_ref, k_ref, v_ref, y_ref, o_scr, l_scr):
    i = pl.program_id(1)
    n_steps = pl.num_programs(1)
    blk = DIL_BLK
    lo = _lane_lo((blk, LANES))
    ii = lax.broadcasted_iota(jnp.int32, (blk, 2 * blk), 0)
    jj = lax.broadcasted_iota(jnp.int32, (blk, 2 * blk), 1)
    band = jnp.logical_and(jj >= ii, jj <= ii + blk)

    for p, dil in enumerate(DILATIONS):
        nb = q_ref.shape[1] // (dil * blk)
        r = i // nb
        cb = i % nb
        start_q = r + dil * blk * cb
        start_p = r + dil * blk * jnp.maximum(cb - 1, 0)
        valid = jnp.logical_and(band, jnp.logical_or(cb > 0, jj >= blk))

        def rows(ref, c, start):
            if dil == 1:
                return ref[c, pl.ds(start, blk), :]
            return ref[c, pl.ds(start, blk, stride=dil), :]

        for c in range(3):
            q = rows(q_ref, c, start_q)
            kwin = jnp.concatenate([rows(k_ref, c, start_p), rows(k_ref, c, start_q)], axis=0).astype(bf16)
            vwin = jnp.concatenate([rows(v_ref, c, start_p), rows(v_ref, c, start_q)], axis=0).astype(bf16)
            outs, lses = [], []
            for half in range(2):
                qm = jnp.where(lo if half == 0 else jnp.logical_not(lo), q, 0.0).astype(bf16)
                s = lax.dot_general(qm, kwin, _NT, preferred_element_type=f32)
                s = jnp.where(valid, s, -jnp.inf)
                mx = jnp.max(s, axis=-1, keepdims=True)
                e = jnp.exp(s - mx)
                tot = jnp.sum(e, axis=-1, keepdims=True)
                lse = mx + jnp.log(tot)
                prob = (e / tot).astype(bf16)
                outs.append(jnp.dot(prob, vwin, preferred_element_type=f32))
                lses.append(lse)
            o_pair = jnp.where(lo, outs[0], outs[1])
            l_pair = jnp.where(lo, lses[0], lses[1])
            if dil == 1:
                o_scr[p, c, pl.ds(start_q, blk), :] = o_pair
                l_scr[p, c, pl.ds(start_q, blk), :] = l_pair
            else:
                o_scr[p, c, pl.ds(start_q, blk, stride=dil), :] = o_pair
                l_scr[p, c, pl.ds(start_q, blk, stride=dil), :] = l_pair

    @pl.when(i == n_steps - 1)
    def _():
        def merge(j, carry):
            sl = pl.ds(pl.multiple_of(j * blk, blk), blk)
            for c in range(3):
                ls = [l_scr[p, c, sl, :] for p in range(3)]
                mx = jnp.maximum(jnp.maximum(ls[0], ls[1]), ls[2])
                es = [jnp.exp(l - mx) for l in ls]
                tot = es[0] + es[1] + es[2]
                acc = (es[0] / tot) * o_scr[0, c, sl, :]
                acc = acc + (es[1] / tot) * o_scr[1, c, sl, :]
                acc = acc + (es[2] / tot) * o_scr[2, c, sl, :]
                y_ref[c, sl, :] = acc
            return carry

        lax.fori_loop(0, q_ref.shape[1] // blk, merge, 0)


def _dil_prompt(p, n_seq, t):
    def grp(gi):
        return pl.BlockSpec((3, t, LANES), lambda b, i: (gi, b, 0))

    n_steps = t // DIL_BLK
    return pl.pallas_call(
        _dil_prompt_kernel,
        grid=(n_seq, n_steps),
        in_specs=[grp(7), grp(8), grp(9)],
        out_specs=pl.BlockSpec((3, t, LANES), lambda b, i: (0, b, 0)),
        out_shape=jax.ShapeDtypeStruct((3, n_seq * t, LANES), f32),
        scratch_shapes=[pltpu.VMEM((3, 3, t, LANES), f32), pltpu.VMEM((3, 3, t, LANES), f32)],
        compiler_params=_cparams(("parallel", "arbitrary")),
        name="dil_prompt",
    )(p, p, p)


N_NEW = 4


def _dil_sample_kernel(q_ref, kn_ref, vn_ref, kc_ref, vc_ref, *rest, n_buf, aliased):
    y_ref, ko_ref, vo_ref = rest[2:] if aliased else rest
    t_all = n_buf + LANES
    zpad = jnp.zeros((LANES - DEC_PAD, LANES), f32)
    qi = lax.broadcasted_iota(jnp.int32, (DEC_PAD, t_all), 0)
    tok = lax.broadcasted_iota(jnp.int32, (DEC_PAD, t_all), 1)
    dist = n_buf + qi - tok
    masks = [jnp.logical_and(jnp.logical_and(dist >= 0, dist <= DIL_BLK * dil), (dist & (dil - 1)) == 0)
             for dil in DILATIONS]
    row_ok = lax.broadcasted_iota(jnp.int32, (DEC_PAD, LANES), 0) < N_NEW
    tail_ok = lax.broadcasted_iota(jnp.int32, (HEAD_DIM, LANES), 1) < N_NEW

    for c in range(3):
        kn_t = jnp.concatenate([kn_ref[c], zpad], axis=0).T
        vn_t = jnp.concatenate([vn_ref[c], zpad], axis=0).T
        q_pair = q_ref[c]
        y_halves = []
        for half in range(2):
            h = 2 * c + half
            k_tail = jnp.where(tail_ok, kn_t[HEAD_DIM * half:HEAD_DIM * (half + 1)], 0.0)
            v_tail = jnp.where(tail_ok, vn_t[HEAD_DIM * half:HEAD_DIM * (half + 1)], 0.0)
            k_all = jnp.concatenate([kc_ref[0, 0, h], k_tail], axis=1)
            v_all = jnp.concatenate([vc_ref[0, 0, h], v_tail], axis=1)
            ko_ref[0, 0, h] = pltpu.roll(k_all, t_all - N_NEW, 1)[:, :n_buf]
            vo_ref[0, 0, h] = pltpu.roll(v_all, t_all - N_NEW, 1)[:, :n_buf]
            q16 = q_pair[:, HEAD_DIM * half:HEAD_DIM * (half + 1)].astype(bf16)
            s = jnp.dot(q16, k_all.astype(bf16), preferred_element_type=f32)
            probs, lses = [], []
            for msk in masks:
                sm = jnp.where(msk, s, -jnp.inf)
                mx = jnp.max(sm, axis=-1, keepdims=True)
                lse = mx + jnp.log(jnp.sum(jnp.exp(sm - mx), axis=-1, keepdims=True))
                probs.append(jnp.exp(sm - lse).astype(bf16))
                lses.append(lse)
            o = lax.dot_general(jnp.concatenate(probs, axis=0), v_all.astype(bf16), _NT,
                                preferred_element_type=f32)
            mx = jnp.maximum(jnp.maximum(lses[0], lses[1]), lses[2])
            es = [jnp.exp(l - mx) for l in lses]
            tot = es[0] + es[1] + es[2]
            y_halves.append((es[0] / tot) * o[0:DEC_PAD] + (es[1] / tot) * o[DEC_PAD:2 * DEC_PAD]
                            + (es[2] / tot) * o[2 * DEC_PAD:3 * DEC_PAD])
        y_ref[c] = jnp.where(row_ok, jnp.concatenate(y_halves, axis=1), 0.0)


def _dil_sample(p, cache_k_t, cache_v_t, layer, prev_outs, n_seq, row0):
    n_buf = cache_k_t.shape[-1]

    def grp(gi):
        return pl.BlockSpec((3, DEC_PAD, LANES), lambda b: (gi, row0 // DEC_PAD + b, 0))

    buf = pl.BlockSpec((1, 1, H_DIL, HEAD_DIM, n_buf), lambda b: (layer, b, 0, 0, 0))
    aliased = prev_outs is not None
    extra = list(prev_outs) if aliased else []
    return pl.pallas_call(
        functools.partial(_dil_sample_kernel, n_buf=n_buf, aliased=aliased),
        grid=(n_seq,),
        in_specs=[grp(7), grp(8), grp(9), buf, buf] + [pl.BlockSpec(memory_space=pl.ANY)] * len(extra),
        out_specs=[pl.BlockSpec((3, DEC_PAD, LANES), lambda b: (0, b, 0)), buf, buf],
        out_shape=[jax.ShapeDtypeStruct((3, n_seq * DEC_PAD, LANES), f32),
                   jax.ShapeDtypeStruct(cache_k_t.shape, f32),
                   jax.ShapeDtypeStruct(cache_v_t.shape, f32)],
        input_output_aliases={5: 1, 6: 2} if aliased else {},
        compiler_params=_cparams(("arbitrary",)),
        name="dil_sample",
    )(p, p, p, cache_k_t, cache_v_t, *extra)


def _layer_norm(z, gamma, beta):
    mu = jnp.mean(z, axis=-1, keepdims=True)
    d = z - mu
    var = jnp.mean(d * d, axis=-1, keepdims=True)
    return d * lax.rsqrt(var + LN_EPS) * gamma + beta


def _merge_kernel(ya_ref, bon_ref, g_ref, yb_ref, yc_ref, x_ref, lnx_ref, seg_ref, wo_ref, ln_ref, o_ref):
    yr = _cat_blocks(ya_ref)
    seg = seg_ref[...]
    mean = _seg_dot(yr, seg)
    d = yr - mean
    var = _seg_dot(d * d, seg)
    ya = d * lax.rsqrt(var + RWKV_GN_EPS) * lnx_ref[0:1, :] + lnx_ref[1:2, :]
    ya = (ya + _cat_blocks(bon_ref)) * _cat_blocks(g_ref)
    ycat = jnp.concatenate([ya, _cat_blocks(yb_ref), _cat_blocks(yc_ref)], axis=1).astype(bf16)
    h = jnp.dot(ycat, wo_ref[...], preferred_element_type=f32)
    o_ref[...] = _layer_norm(DEEPNORM_ALPHA * x_ref[...] + h, ln_ref[0:1, :], ln_ref[1:2, :])


def _merge_prompt(ya_tm, bonus, g, yb, yc, x, params, n_seq, t, tm):
    nt = t // tm
    tok = pl.BlockSpec((3, tm, LANES), lambda b, ti: (0, b * nt + ti, 0))
    rows = pl.BlockSpec((tm, D_MODEL), lambda b, ti: (b * nt + ti, 0))

    def full(a):
        return pl.BlockSpec(a.shape, lambda b, ti: (0,) * a.ndim)

    return pl.pallas_call(
        _merge_kernel,
        grid=(n_seq, nt),
        in_specs=[pl.BlockSpec((3, tm, LANES), lambda b, ti: (0, ti, b)), tok, tok, tok, tok, rows]
        + [full(a) for a in params],
        out_specs=rows,
        out_shape=jax.ShapeDtypeStruct(x.shape, f32),
        compiler_params=_cparams(("parallel", "parallel")),
        name="mixer_merge_prompt",
    )(ya_tm, bonus, g, yb, yc, x, *params)


def _merge_rows(ya, bonus, g, yb, yc, x, params, tm):
    blk = pl.BlockSpec((3, tm, LANES), lambda i: (0, i, 0))
    rows = pl.BlockSpec((tm, D_MODEL), lambda i: (i, 0))

    def full(a):
        return pl.BlockSpec(a.shape, lambda i: (0,) * a.ndim)

    return pl.pallas_call(
        _merge_kernel,
        grid=(x.shape[0] // tm,),
        in_specs=[blk] * 5 + [rows] + [full(a) for a in params],
        out_specs=rows,
        out_shape=jax.ShapeDtypeStruct(x.shape, f32),
        compiler_params=_cparams(("parallel",)),
        name="mixer_merge_rows",
    )(ya, bonus, g, yb, yc, x, *params)


EXPERT_BLOCK = 8


def _first_max(x, idx, axes, sentinel):
    mx = x
    for ax in axes:
        mx = jnp.max(mx, axis=ax, keepdims=True)
    am = jnp.where(x == mx, idx, sentinel)
    for ax in axes:
        am = jnp.min(am, axis=ax, keepdims=True)
    return mx, am


def _route(logits_t, bias_col):
    n_tok = logits_t.shape[1]
    per = N_EXPERTS // N_ROUTE_GROUPS
    scores = jax.nn.sigmoid(logits_t)
    shp = (N_ROUTE_GROUPS, per, n_tok)
    s3 = scores.reshape(shp)
    b3 = (scores + bias_col).reshape(shp)
    sub = lax.broadcasted_iota(jnp.int32, shp, 1)
    grp = lax.broadcasted_iota(jnp.int32, shp, 0)
    eid = grp * per + sub
    neg = -jnp.inf

    m1, i1 = _first_max(b3, sub, (1,), per)
    m2, _ = _first_max(jnp.where(sub == i1, neg, b3), sub, (1,), per)
    gscore = m1 + m2
    gid = lax.broadcasted_iota(jnp.int32, gscore.shape, 0)
    gsel = jnp.zeros(gscore.shape, jnp.bool_)
    for _ in range(TOPK_GROUPS):
        _, gi = _first_max(gscore, gid, (0,), N_ROUTE_GROUPS)
        hit = gid == gi
        gsel = jnp.logical_or(gsel, hit)
        gscore = jnp.where(hit, neg, gscore)
    masked = jnp.where(gsel, b3, neg)
    esel = jnp.zeros(shp, jnp.bool_)
    for _ in range(TOP_K):
        _, ei = _first_max(masked, eid, (1, 0), N_EXPERTS)
        hit = eid == ei
        esel = jnp.logical_or(esel, hit)
        masked = jnp.where(hit, neg, masked)
    top_s = jnp.where(esel, s3, 0.0)
    denom = jnp.sum(jnp.sum(top_s, axis=1, keepdims=True), axis=0, keepdims=True)
    return (top_s / denom * ROUTE_SCALE).reshape(N_EXPERTS, n_tok)


def _moe_kernel(x_ref, wr_ref, rb_ref, wg_ref, wu_ref, wd_ref, wsg_ref, wsu_ref, wsd_ref, ln_ref,
                o_ref, x16_scr, comb_scr, acc_scr):
    j = pl.program_id(1)

    @pl.when(j == 0)
    def _():
        x16 = x_ref[...].astype(bf16)
        x16_scr[...] = x16
        logits_t = lax.dot_general(wr_ref[...], x16, _NT, preferred_element_type=f32)
        comb_t = _route(logits_t, rb_ref[...])
        comb_t = jnp.concatenate([comb_t, jnp.zeros_like(comb_t)], axis=0)
        comb = comb_t.T
        comb_scr[0] = comb
        for jj in range(1, N_EXPERTS // EXPERT_BLOCK):
            comb_scr[jj] = pltpu.roll(comb, LANES - EXPERT_BLOCK * jj, 1)
        sg = jnp.dot(x16, wsg_ref[...], preferred_element_type=f32)
        su = jnp.dot(x16, wsu_ref[...], preferred_element_type=f32)
        hs = (sg * jax.nn.sigmoid(sg) * su).astype(bf16)
        acc_scr[...] = jnp.dot(hs, wsd_ref[...], preferred_element_type=f32)

    x16 = x16_scr[...]
    gate = jnp.dot(x16, wg_ref[...], preferred_element_type=f32)
    up = jnp.dot(x16, wu_ref[...], preferred_element_type=f32)
    comb = comb_scr[j]
    tm = comb.shape[0]
    cexp = jnp.concatenate([jnp.broadcast_to(comb[:, e:e + 1], (tm, D_EXPERT)) for e in range(EXPERT_BLOCK)], axis=1)
    h = (gate * jax.nn.sigmoid(gate) * up * cexp).astype(bf16)
    acc_scr[...] += jnp.dot(h, wd_ref[...], preferred_element_type=f32)

    @pl.when(j == pl.num_programs(1) - 1)
    def _():
        o_ref[...] = _layer_norm(DEEPNORM_ALPHA * x_ref[...] + acc_scr[...], ln_ref[0:1, :], ln_ref[1:2, :])


def _moe(x, wr_t, rb_col, wg, wu, wd, wsg, wsu, wsd, ln, tm):
    m = x.shape[0]
    nb = EXPERT_BLOCK * D_EXPERT

    def full(a):
        return pl.BlockSpec(a.shape, lambda i, j: (0,) * a.ndim)

    return pl.pallas_call(
        _moe_kernel,
        grid=(m // tm, N_EXPERTS // EXPERT_BLOCK),
        in_specs=[pl.BlockSpec((tm, D_MODEL), lambda i, j: (i, 0)), full(wr_t), full(rb_col),
                  pl.BlockSpec((D_MODEL, nb), lambda i, j: (0, j)),
                  pl.BlockSpec((D_MODEL, nb), lambda i, j: (0, j)),
                  pl.BlockSpec((nb, D_MODEL), lambda i, j: (j, 0)),
                  full(wsg), full(wsu), full(wsd), full(ln)],
        out_specs=pl.BlockSpec((tm, D_MODEL), lambda i, j: (i, 0)),
        out_shape=jax.ShapeDtypeStruct((m, D_MODEL), f32),
        scratch_shapes=[pltpu.VMEM((tm, D_MODEL), bf16),
                        pltpu.VMEM((N_EXPERTS // EXPERT_BLOCK, tm, LANES), f32),
                        pltpu.VMEM((tm, D_MODEL), f32)],
        compiler_params=_cparams(("parallel", "arbitrary")),
        name="moe",
    )(x, wr_t, rb_col, wg, wu, wd, wsg, wsu, wsd, ln)


def _pad_heads(a, n_heads):
    return jnp.pad(a, [(0, 0)] * (a.ndim - 1) + [(0, GROUP - n_heads * HEAD_DIM)])


def _pad_w_in(w):
    widths = [C_RWKV] * 3 + [C_RET] * 4 + [C_DIL] * 3
    parts, off = [], 0
    for wd in widths:
        parts.append(jnp.pad(w[:, off:off + wd], ((0, 0), (0, GROUP - wd))))
        off += wd
    return jnp.concatenate(parts, axis=1).astype(bf16)


def _rope_tables(pos, inv_freq):
    ang = pos.astype(f32)[:, None] * inv_freq[None, :]
    cos, sin = jnp.cos(ang), jnp.sin(ang)
    return jnp.tile(jnp.concatenate([cos, cos], axis=1), (1, 2)), jnp.tile(jnp.concatenate([-sin, sin], axis=1), (1, 2))


def _segment_matrix(width, scale):
    idx = np.arange(width) // HEAD_DIM
    return jnp.asarray((idx[:, None] == idx[None, :]).astype(np.float32) * scale)


def _pairs_from_heads(s):
    b = s.shape[0]
    s6 = jnp.pad(s, ((0, 0), (0, 1), (0, 0), (0, 0))).reshape(b, 3, 2, HEAD_DIM, HEAD_DIM)
    z = jnp.zeros((b, 3, HEAD_DIM, HEAD_DIM), s.dtype)
    top = jnp.concatenate([s6[:, :, 0], z], axis=-1)
    bot = jnp.concatenate([z, s6[:, :, 1]], axis=-1)
    return jnp.concatenate([top, bot], axis=-2)


def _heads_from_pairs(sp):
    b = sp.shape[0]
    h0 = sp[:, :, :HEAD_DIM, :HEAD_DIM]
    h1 = sp[:, :, HEAD_DIM:, HEAD_DIM:]
    return jnp.stack([h0, h1], axis=2).reshape(b, 6, HEAD_DIM, HEAD_DIM)[:, :H_RET]


def _rwkv_pairs_from_heads(s):
    b = s.shape[0]
    s6 = jnp.pad(s, ((0, 0), (0, 1), (0, 0), (0, 0))).reshape(b, 3, 2, HEAD_DIM, HEAD_DIM)
    return jnp.transpose(s6, (1, 0, 3, 2, 4)).reshape(3, b, HEAD_DIM, LANES)


def _rwkv_heads_from_pairs(sp):
    b = sp.shape[1]
    s6 = jnp.transpose(sp.reshape(3, b, HEAD_DIM, 2, HEAD_DIM), (1, 0, 3, 2, 4))
    return s6.reshape(b, 6, HEAD_DIM, HEAD_DIM)[:, :H_RWKV]


def kernel(x_prompt, x_sample, state_rwkv, state_shift, state_ret, cache_k_win, cache_v_win, w_in, mu_rkv, mu_wag, w0, w1, w2, a0, a1, a2, g1, g2, k_k, k_a, r_k, lnx_g, lnx_b, w_out, ln1_g, ln1_b, w_router, router_bias, w_gate, w_up, w_down, ws_gate, ws_up, ws_down, ln2_g, ln2_b):
    bp, tp, _ = x_prompt.shape
    bs, ts, _ = x_sample.shape
    n_buf = cache_k_win.shape[2]
    depth = w_in.shape[0]
    assert ts == N_NEW and n_buf == DILATIONS[-1] * DIL_BLK and tp % (DILATIONS[-1] * DIL_BLK) == 0
    mp = bp * tp
    ms = bs * DEC_PAD
    tm, tm_moe = ROW_TILE, MOE_ROW_TILE
    assert tp % tm == 0 and ms % tm_moe == 0 and mp % tm_moe == 0 and bs % SEQ_BLOCK == 0

    inv_ret = 1.0 / (10000.0 ** jnp.linspace(0.0, 1.0, HEAD_DIM // 2, dtype=f32))
    inv_dil = ROPE_THETA ** (-jnp.arange(0, HEAD_DIM, 2, dtype=f32) / HEAD_DIM)
    pos_p = jnp.tile(jnp.arange(tp), bp)
    pos_s = jnp.tile(PAST_LEN + jnp.arange(DEC_PAD), bs)
    tabs_p = _rope_tables(pos_p, inv_ret) + _rope_tables(pos_p, inv_dil)
    tabs_s = _rope_tables(pos_s, inv_ret) + _rope_tables(pos_s, inv_dil)
    zero_tab = jnp.zeros((bs, LANES), f32)

    seg_sum = _segment_matrix(GROUP, 1.0).astype(bf16)
    seg_mean = _segment_matrix(GROUP, 1.0 / HEAD_DIM).astype(bf16)
    seg_mean_pair = _segment_matrix(LANES, 1.0 / HEAD_DIM).astype(bf16)
    seg_pair_bf16 = _segment_matrix(LANES, 1.0).astype(bf16)

    x_p = x_prompt.reshape(mp, D_MODEL)
    x_s = jnp.pad(x_sample, ((0, 0), (0, DEC_PAD - ts), (0, 0))).reshape(ms, D_MODEL)

    cache_k_t = jnp.transpose(cache_k_win, (0, 1, 3, 4, 2))
    cache_v_t = jnp.transpose(cache_v_win, (0, 1, 3, 4, 2))
    win_outs = None

    outs_p = [[] for _ in range(5)]
    outs_s = [[] for _ in range(3)]
    for l in range(depth):
        xp3 = x_p.reshape(bp, tp, D_MODEL)
        xs3 = x_s.reshape(bs, DEC_PAD, D_MODEL)
        x_last_s = state_shift[l]
        outs_p[1].append(xp3[:, -1])
        outs_s[1].append(xs3[:, ts - 1])

        w_in_p = _pad_w_in(w_in[l])
        p_p, k_new_t, v_new_t = _inproj(x_p, w_in_p, *tabs_p, tm, seq_len=tp)
        p_s, = _inproj(x_s, w_in_p, *tabs_s, tm)
        p_last, = _inproj(x_last_s, w_in_p, zero_tab, zero_tab, zero_tab, zero_tab, min(bs, 128))

        xs_prev = jnp.concatenate([x_last_s[:, None], xs3[:, :-1]], axis=1).reshape(ms, D_MODEL)
        pr_s = p_s[:9].reshape(9, bs, DEC_PAD, LANES)
        pp_s = jnp.concatenate([p_last[:9][:, :, None], pr_s[:, :, :-1]], axis=2).reshape(9, ms, LANES)

        vec = jnp.stack([_pad_heads(v, H_RWKV) for v in (w0[l], a0[l], k_k[l], k_a[l], r_k[l].reshape(-1))])
        prep_params = (_pad_heads(mu_rkv[l], H_RWKV), mu_wag[l], vec,
                       w1[l].astype(bf16), _pad_heads(w2[l], H_RWKV).astype(bf16),
                       a1[l].astype(bf16), _pad_heads(a2[l], H_RWKV).astype(bf16),
                       g1[l].astype(bf16), _pad_heads(g2[l], H_RWKV).astype(bf16), seg_sum)
        prep_p = _rwkv_prep_prompt(x_p, p_p, prep_params, bp, tp, tm)
        prep_s = _rwkv_prep_rows(x_s, xs_prev, p_s, pp_s, prep_params, 0, ms, tm)

        ops_s = [jnp.swapaxes(a.reshape(3, bs, DEC_PAD, LANES), 1, 2).reshape(3, DEC_PAD, bs * LANES)
                 for a in prep_s[:7]]
        ya_p, s_rwkv_p = _rwkv_scan(list(prep_p[:7]), jnp.zeros((3, bp, HEAD_DIM, LANES), f32), seg_pair_bf16,
                                    bp, SCAN_CHUNK, SCAN_CHUNK)
        ya_s, s_rwkv_s = _rwkv_scan(ops_s, _rwkv_pairs_from_heads(state_rwkv[l]), seg_pair_bf16,
                                    SEQ_BLOCK, DEC_PAD, ts)
        ya_s = jnp.swapaxes(ya_s.reshape(3, DEC_PAD, bs, LANES), 1, 2).reshape(3, ms, LANES)
        outs_p[0].append(_rwkv_heads_from_pairs(s_rwkv_p))
        outs_s[0].append(_rwkv_heads_from_pairs(s_rwkv_s))

        yb_p, s_ret_p = _retention(p_p, jnp.zeros((bp, 3, LANES, LANES), f32), seg_mean_pair,
                                   bp, 1, RET_CHUNK, tp // RET_CHUNK, RET_CHUNK, 0)
        yb_s, s_ret_s = _retention(p_s, _pairs_from_heads(state_ret[l]), seg_mean_pair,
                                   bs, SEQ_BLOCK, DEC_PAD, 1, ts, 0)
        outs_p[2].append(_heads_from_pairs(s_ret_p))
        outs_s[2].append(_heads_from_pairs(s_ret_s))

        yc_p = _dil_prompt(p_p, bp, tp)
        yc_s, k_win_t, v_win_t = _dil_sample(p_s, cache_k_t, cache_v_t, l, win_outs, bs, 0)
        win_outs = (k_win_t, v_win_t)
        keep = min(n_buf, tp)
        outs_p[3].append(k_new_t[..., tp - keep:])
        outs_p[4].append(v_new_t[..., tp - keep:])

        w_out_p = jnp.concatenate([
            jnp.pad(w_out[l][:C_RWKV], ((0, GROUP - C_RWKV), (0, 0))),
            jnp.pad(w_out[l][C_RWKV:C_RWKV + C_RET], ((0, GROUP - C_RET), (0, 0))),
            w_out[l][C_RWKV + C_RET:]], axis=0).astype(bf16)
        lnx = jnp.stack([_pad_heads(lnx_g[l], H_RWKV), _pad_heads(lnx_b[l], H_RWKV)])
        merge_params = (lnx, seg_mean, w_out_p, jnp.stack([ln1_g[l], ln1_b[l]]))
        x_p = _merge_prompt(ya_p, prep_p[8], prep_p[7], yb_p, yc_p, x_p, merge_params, bp, tp, tm)
        x_s = _merge_rows(ya_s, prep_s[8], prep_s[7], yb_s, yc_s, x_s, merge_params, tm)

        wg = jnp.transpose(w_gate[l], (1, 0, 2)).reshape(D_MODEL, N_EXPERTS * D_EXPERT).astype(bf16)
        wu = jnp.transpose(w_up[l], (1, 0, 2)).reshape(D_MODEL, N_EXPERTS * D_EXPERT).astype(bf16)
        wd = w_down[l].reshape(N_EXPERTS * D_EXPERT, D_MODEL).astype(bf16)
        moe_params = (w_router[l].T.astype(bf16), router_bias[l].reshape(N_EXPERTS, 1), wg, wu, wd,
                      ws_gate[l].astype(bf16), ws_up[l].astype(bf16), ws_down[l].astype(bf16),
                      jnp.stack([ln2_g[l], ln2_b[l]]))
        x_p = _moe(x_p, *moe_params, tm_moe)
        x_s = _moe(x_s, *moe_params, tm_moe)

    y_prompt = x_p.reshape(bp, tp, D_MODEL)
    y_sample = x_s.reshape(bs, DEC_PAD, D_MODEL)[:, :ts]
    wins_p = tuple(jnp.transpose(jnp.stack(o), (0, 1, 4, 2, 3)) for o in outs_p[3:])
    wins_s = tuple(jnp.transpose(w, (0, 1, 4, 2, 3)) for w in win_outs)
    return ((y_prompt, y_sample) + tuple(jnp.stack(o) for o in outs_p[:3]) + wins_p
            + tuple(jnp.stack(o) for o in outs_s) + wins_s)
```

```python
import functools

import numpy as np
import jax
import jax.numpy as jnp
from jax import lax
from jax.experimental import pallas as pl
from jax.experimental.pallas import tpu as pltpu

f32 = jnp.float32
bf16 = jnp.bfloat16

D_MODEL = 1024
HEAD_DIM = 64
LANES = 128
SUBLANES = 8
H_RWKV = 5
H_RET = 5
H_DIL = 6
GROUP = 384
N_GROUPS_IN = 10
N_BLOCKS_IN = 3 * N_GROUPS_IN
C_RWKV = H_RWKV * HEAD_DIM
C_RET = H_RET * HEAD_DIM
C_DIL = H_DIL * HEAD_DIM
PAST_LEN = 2048
DEC_PAD = SUBLANES
RET_CHUNK = 128
RWKV_GN_EPS = 64e-5
RET_GN_EPS = 1e-5
DILATIONS = (1, 4, 16)
DIL_BLK = 128
ROPE_THETA = 10000.0
N_EXPERTS = 64
D_EXPERT = 128
TOP_K = 8
N_ROUTE_GROUPS = 8
TOPK_GROUPS = 4
ROUTE_SCALE = 2.5
LN_EPS = 1e-5
DEPTH = 2
DEEPNORM_ALPHA = (2 * DEPTH) ** 0.25
VMEM_LIMIT = 56 * 1024 * 1024

ROW_TILE = 256
MOE_ROW_TILE = 512
SCAN_CHUNK = 64
SEQ_BLOCK = 8

_NT = (((1,), (1,)), ((), ()))
_TN = (((0,), (0,)), ((), ()))


def _cparams(sem):
    return pltpu.CompilerParams(dimension_semantics=sem, vmem_limit_bytes=VMEM_LIMIT)


def _lane_lo(shape):
    return (lax.broadcasted_iota(jnp.int32, shape, len(shape) - 1) % LANES) < HEAD_DIM


def _cat_blocks(ref):
    return jnp.concatenate([ref[0], ref[1], ref[2]], axis=1)


def _put_blocks(ref, val):
    for c in range(3):
        ref[c] = val[:, LANES * c:LANES * (c + 1)]


def _inproj_kernel(x_ref, w_ref, cr_ref, sr_ref, cd_ref, sd_ref, p_ref, *win_refs):
    acc = jnp.dot(x_ref[...].astype(bf16), w_ref[...], preferred_element_type=f32)
    tm = acc.shape[0]
    first = (lax.broadcasted_iota(jnp.int32, (tm, LANES), 1) % HEAD_DIM) < HEAD_DIM // 2

    def rot(xb, c, s):
        partner = jnp.where(first, pltpu.roll(xb, LANES - HEAD_DIM // 2, 1), pltpu.roll(xb, HEAD_DIM // 2, 1))
        return xb * c + partner * s

    for j in range(N_BLOCKS_IN):
        blk = acc[:, LANES * j:LANES * (j + 1)]
        g = j // 3
        if g in (3, 4):
            blk = rot(blk, cr_ref[...], sr_ref[...])
        if g in (7, 8):
            blk = rot(blk, cd_ref[...], sd_ref[...])
        if g in (4, 7):
            blk = blk * (HEAD_DIM ** -0.5)
        p_ref[j] = blk
        if win_refs and g in (8, 9):
            c = j % 3
            win_refs[g - 8][0, 2 * c:2 * c + 2] = blk.T.reshape(2, HEAD_DIM, tm)


def _inproj(x, w, cr, sr, cd, sd, tm, seq_len=None):
    m = x.shape[0]
    tab = pl.BlockSpec((tm, LANES), lambda i: (i, 0))
    out_specs = [pl.BlockSpec((N_BLOCKS_IN, tm, LANES), lambda i: (0, i, 0))]
    out_shape = [jax.ShapeDtypeStruct((N_BLOCKS_IN, m, LANES), f32)]
    if seq_len is not None:
        nt = seq_len // tm
        win = pl.BlockSpec((1, H_DIL, HEAD_DIM, tm), lambda i: (i // nt, 0, 0, i % nt))
        out_specs += [win, win]
        out_shape += [jax.ShapeDtypeStruct((m // seq_len, H_DIL, HEAD_DIM, seq_len), f32)] * 2
    return pl.pallas_call(
        _inproj_kernel,
        grid=(m // tm,),
        in_specs=[pl.BlockSpec((tm, D_MODEL), lambda i: (i, 0)),
                  pl.BlockSpec((D_MODEL, N_BLOCKS_IN * LANES), lambda i: (0, 0)),
                  tab, tab, tab, tab],
        out_specs=out_specs,
        out_shape=out_shape,
        compiler_params=_cparams(("parallel",)),
        name="inproj",
    )(x, w, cr, sr, cd, sd)


def _softplus(z):
    return jnp.maximum(z, 0.0) + jnp.log(1.0 + jnp.exp(-jnp.abs(z)))


def _rwkv_prep_core(x, x_prev, rkv, rkv_prev,
                    mu_rkv_ref, mu_wag_ref, vec_ref, w1_ref, w2_ref, a1_ref, a2_ref, g1_ref, g2_ref, seg_ref,
                    wo_ref, ko_ref, vo_ref, kko_ref, bo_ref, uo_ref, kro_ref, go_ref, bon_ref):
    xx = x_prev - x
    xw = (x + xx * mu_wag_ref[0:1, :]).astype(bf16)
    xa = (x + xx * mu_wag_ref[1:2, :]).astype(bf16)
    xg = (x + xx * mu_wag_ref[2:3, :]).astype(bf16)
    w0, a0, k_k, k_a, r_k = (vec_ref[i:i + 1, :] for i in range(5))

    lw = jnp.tanh(jnp.dot(xw, w1_ref[...], preferred_element_type=f32))
    wl = w0 + jnp.dot(lw.astype(bf16), w2_ref[...], preferred_element_type=f32)
    w_log = -_softplus(-wl) - 0.5
    decay = jnp.exp(-jnp.exp(w_log))
    la = jnp.dot(xa, a1_ref[...], preferred_element_type=f32)
    a = jax.nn.sigmoid(a0 + jnp.dot(la.astype(bf16), a2_ref[...], preferred_element_type=f32))
    lg = jax.nn.sigmoid(jnp.dot(xg, g1_ref[...], preferred_element_type=f32))
    g = jnp.dot(lg.astype(bf16), g2_ref[...], preferred_element_type=f32)

    def shifted(i):
        cur = rkv[:, GROUP * i:GROUP * (i + 1)]
        return cur + mu_rkv_ref[i:i + 1, :] * (rkv_prev[:, GROUP * i:GROUP * (i + 1)] - cur)

    r = shifted(0)
    k = shifted(1)
    v = shifted(2)

    seg = seg_ref[...]
    kk = k * k_k
    ss = _seg_dot(kk * kk, seg)
    kk = kk * lax.rsqrt(jnp.maximum(ss, 1e-24))
    k = k * (1.0 + (a - 1.0) * k_a)
    bonus = _seg_dot(r * k * r_k, seg) * v
    b = kk * a
    b_dot_r = _seg_dot(b * r, seg)
    k_dot_r = _seg_dot(k * r, seg)

    _put_blocks(wo_ref, decay)
    _put_blocks(ko_ref, k)
    _put_blocks(vo_ref, v)
    _put_blocks(kko_ref, kk)
    _put_blocks(bo_ref, b)
    _put_blocks(uo_ref, decay * r - kk * b_dot_r)
    _put_blocks(kro_ref, k_dot_r)
    _put_blocks(go_ref, g)
    _put_blocks(bon_ref, bonus)


def _rwkv_prep_seq_kernel(x_ref, r_ref, k_ref, v_ref, *rest):
    params, outs, (x_carry, rkv_carry) = rest[:10], rest[10:19], rest[19:]
    ti = pl.program_id(1)

    @pl.when(ti == 0)
    def _():
        x_carry[...] = jnp.zeros(x_carry.shape, f32)
        rkv_carry[...] = jnp.zeros(rkv_carry.shape, f32)

    x = x_ref[...]
    rkv = jnp.concatenate([_cat_blocks(r_ref), _cat_blocks(k_ref), _cat_blocks(v_ref)], axis=1)
    tm = x.shape[0]

    def prev_rows(cur, carry):
        first = lax.broadcasted_iota(jnp.int32, cur.shape, 0) == 0
        prev = jnp.where(first, carry[SUBLANES - 1:SUBLANES, :], pltpu.roll(cur, 1, 0))
        carry[...] = cur[tm - SUBLANES:tm]
        return prev

    _rwkv_prep_core(x, prev_rows(x, x_carry), rkv, prev_rows(rkv, rkv_carry), *params, *outs)


def _rwkv_prep_rows_kernel(x_ref, xp_ref, r_ref, k_ref, v_ref, rp_ref, kp_ref, vp_ref, *rest):
    rkv = jnp.concatenate([_cat_blocks(r_ref), _cat_blocks(k_ref), _cat_blocks(v_ref)], axis=1)
    rkv_prev = jnp.concatenate([_cat_blocks(rp_ref), _cat_blocks(kp_ref), _cat_blocks(vp_ref)], axis=1)
    _rwkv_prep_core(x_ref[...], xp_ref[...], rkv, rkv_prev, *rest)


def _rwkv_prep_prompt(x, p, params, n_seq, t, tm):
    nt = t // tm

    def grp(gi):
        return pl.BlockSpec((3, tm, LANES), lambda b, ti: (gi, b * nt + ti, 0))

    def full(a):
        return pl.BlockSpec(a.shape, lambda b, ti: (0,) * a.ndim)

    scan_out = jax.ShapeDtypeStruct((3, t, n_seq * LANES), f32)
    tok_out = jax.ShapeDtypeStruct((3, n_seq * t, LANES), f32)
    return pl.pallas_call(
        _rwkv_prep_seq_kernel,
        grid=(n_seq, nt),
        in_specs=[pl.BlockSpec((tm, D_MODEL), lambda b, ti: (b * nt + ti, 0)), grp(0), grp(1), grp(2)]
        + [full(a) for a in params],
        out_specs=[pl.BlockSpec((3, tm, LANES), lambda b, ti: (0, ti, b))] * 7
        + [pl.BlockSpec((3, tm, LANES), lambda b, ti: (0, b * nt + ti, 0))] * 2,
        out_shape=[scan_out] * 7 + [tok_out] * 2,
        scratch_shapes=[pltpu.VMEM((SUBLANES, D_MODEL), f32), pltpu.VMEM((SUBLANES, 3 * GROUP), f32)],
        compiler_params=_cparams(("parallel", "arbitrary")),
        name="rwkv_prep_prompt",
    )(x, p, p, p, *params)


def _rwkv_prep_rows(x, xp, p, pp, params, row0, n_rows, tm):
    i0 = row0 // tm

    def grp(gi):
        return pl.BlockSpec((3, tm, LANES), lambda i: (gi, i0 + i, 0))

    def grp_prev(gi):
        return pl.BlockSpec((3, tm, LANES), lambda i: (gi, i, 0))

    def full(a):
        return pl.BlockSpec(a.shape, lambda i: (0,) * a.ndim)

    out = jax.ShapeDtypeStruct((3, n_rows, LANES), f32)
    return pl.pallas_call(
        _rwkv_prep_rows_kernel,
        grid=(n_rows // tm,),
        in_specs=[pl.BlockSpec((tm, D_MODEL), lambda i: (i0 + i, 0)), pl.BlockSpec((tm, D_MODEL), lambda i: (i, 0)),
                  grp(0), grp(1), grp(2), grp_prev(0), grp_prev(1), grp_prev(2)] + [full(a) for a in params],
        out_specs=[pl.BlockSpec((3, tm, LANES), lambda i: (0, i, 0))] * 9,
        out_shape=[out] * 9,
        compiler_params=_cparams(("parallel",)),
        name="rwkv_prep_rows",
    )(x, xp, p, p, p, pp, pp, pp, *params)


def _rwkv_scan_kernel(w_ref, k_ref, v_ref, kk_ref, b_ref, u_ref, kr_ref, s0_ref, seg_ref, y_ref, so_ref, s_scr,
                      *, n_steps):
    ti = pl.program_id(1)
    bb = w_ref.shape[2] // LANES
    rows = 3 * bb * HEAD_DIM

    @pl.when(ti == 0)
    def _():
        s_scr[...] = s0_ref[...].reshape(rows, LANES)

    if n_steps < y_ref.shape[1]:
        y_ref[...] = jnp.zeros(y_ref.shape, f32)
    eye = ((lax.broadcasted_iota(jnp.int32, (rows, LANES), 0) % HEAD_DIM)
           == (lax.broadcasted_iota(jnp.int32, (rows, LANES), 1) % HEAD_DIM))
    seg = seg_ref[...]

    def rows_of(ref, t):
        tiles = []
        for c in range(3):
            row = ref[c, pl.ds(t, 1), :]
            tiles += [jnp.broadcast_to(row[:, LANES * b:LANES * (b + 1)], (HEAD_DIM, LANES)) for b in range(bb)]
        return jnp.concatenate(tiles, axis=0)

    def step(t, carry):
        s = s_scr[...]
        p = (s * rows_of(kk_ref, t)).astype(bf16)
        q = (s * rows_of(u_ref, t)).astype(bf16)
        vm = jnp.where(eye, rows_of(v_ref, t), 0.0)
        vhi = vm.astype(bf16)
        vlo = (vm - vhi.astype(f32)).astype(bf16)
        red1 = jnp.dot(jnp.concatenate([p, q], axis=0), seg, preferred_element_type=f32)
        red2 = jnp.dot(jnp.concatenate([vhi, vlo], axis=0), seg, preferred_element_type=f32)
        sa = red1[0:rows]
        vcol = red2[0:rows] + red2[rows:2 * rows]
        ycol = red1[rows:2 * rows] + vcol * rows_of(kr_ref, t)
        s_scr[...] = s * rows_of(w_ref, t) - sa * rows_of(b_ref, t) + vcol * rows_of(k_ref, t)
        ym = jnp.where(eye, ycol, 0.0)
        for c in range(3):
            y_ref[c, pl.ds(t, 1), :] = jnp.concatenate(
                [jnp.sum(ym[(c * bb + b) * HEAD_DIM:(c * bb + b + 1) * HEAD_DIM], axis=0, keepdims=True)
                 for b in range(bb)], axis=1)
        return carry

    lax.fori_loop(0, n_steps, step, 0)

    @pl.when(ti == pl.num_programs(1) - 1)
    def _():
        so_ref[...] = s_scr[...].reshape(so_ref.shape)


def _rwkv_scan(ops, s0, seg, bb, tc, n_steps):
    _, t, lanes = ops[0].shape
    b = lanes // LANES
    blk = pl.BlockSpec((3, tc, bb * LANES), lambda bi, ti: (0, ti, bi))
    st = pl.BlockSpec((3, bb, HEAD_DIM, LANES), lambda bi, ti: (0, bi, 0, 0))
    return pl.pallas_call(
        functools.partial(_rwkv_scan_kernel, n_steps=n_steps),
        grid=(b // bb, t // tc),
        in_specs=[blk] * 7 + [st, pl.BlockSpec((LANES, LANES), lambda bi, ti: (0, 0))],
        out_specs=[blk, st],
        out_shape=[jax.ShapeDtypeStruct((3, t, b * LANES), f32),
                   jax.ShapeDtypeStruct((3, b, HEAD_DIM, LANES), f32)],
        scratch_shapes=[pltpu.VMEM((3 * bb * HEAD_DIM, LANES), f32)],
        compiler_params=_cparams(("parallel", "arbitrary")),
        name="rwkv_scan",
    )(*ops, s0, seg)


def _log_gamma(h):
    return float(np.log(np.float32(1.0) - np.float32(2.0) ** np.float32(-5.0 - min(h, H_RET - 1))))


def _seg_dot(x, seg16):
    hi = x.astype(bf16)
    lo = (x - hi.astype(f32)).astype(bf16)
    return (jnp.dot(hi, seg16, preferred_element_type=f32) + jnp.dot(lo, seg16, preferred_element_type=f32))


def _retention_kernel(q_ref, k_ref, v_ref, g_ref, s0_ref, seg_ref, y_ref, so_ref, s_scr, *, rows, n_valid):
    ci = pl.program_id(1)
    n_seq = q_ref.shape[1] // rows

    @pl.when(ci == 0)
    def _():
        s_scr[...] = s0_ref[...]

    lo = _lane_lo((rows, LANES))
    ii = lax.broadcasted_iota(jnp.int32, (rows, rows), 0)
    jj = lax.broadcasted_iota(jnp.int32, (rows, rows), 1)
    rel = (ii - jj).astype(f32)
    causal = ii >= jj
    pos = lax.broadcasted_iota(jnp.int32, (rows, LANES), 0)
    posf = pos.astype(f32)
    row_ok = pos < n_valid
    sq_r = lax.broadcasted_iota(jnp.int32, (LANES, LANES), 0) < HEAD_DIM
    sq_c = lax.broadcasted_iota(jnp.int32, (LANES, LANES), 1) < HEAD_DIM
    same_head = sq_r == sq_c
    seg = seg_ref[...]
    items = [(sq, c) for sq in range(n_seq) for c in range(3)]
    decays = {}
    for c in range(3):
        lg0, lg1 = _log_gamma(2 * c), _log_gamma(2 * c + 1)
        lg_lane = jnp.where(lo, lg0, lg1)
        decays[c] = dict(
            dmat=[jnp.where(causal, jnp.exp(lg * jnp.maximum(rel, 0.0)), 0.0) for lg in (lg0, lg1)],
            cross=jnp.exp(lg_lane * (posf + 1.0)),
            k_in=jnp.exp(lg_lane * (float(n_valid) - 1.0 - posf)),
            state=jnp.exp(jnp.where(sq_c, lg0, lg1) * float(n_valid)))

    st1 = {}
    for sq, c in items:
        sl = slice(sq * rows, (sq + 1) * rows)
        q = q_ref[c, sl, :]
        k = jnp.where(row_ok, k_ref[c, sl, :], 0.0)
        k16 = k.astype(bf16)
        v16 = v_ref[c, sl, :].astype(bf16)
        scores = [lax.dot_general(jnp.where(lo if half == 0 else jnp.logical_not(lo), q, 0.0).astype(bf16), k16, _NT,
                                  preferred_element_type=f32) for half in range(2)]
        state = s_scr[sq, c]
        cross = jnp.dot(q.astype(bf16), state.astype(bf16), preferred_element_type=f32)
        upd = lax.dot_general((k * decays[c]["k_in"]).astype(bf16), v16, _TN, preferred_element_type=f32)
        s_scr[sq, c] = state * decays[c]["state"] + jnp.where(same_head, upd, 0.0)
        st1[sq, c] = (scores, cross, v16)

    outs = {}
    for sq, c in items:
        scores, cross, v16 = st1[sq, c]
        o_halves = [jnp.dot((scores[half] * decays[c]["dmat"][half]).astype(bf16), v16, preferred_element_type=f32)
                    for half in range(2)]
        outs[sq, c] = jnp.where(lo, o_halves[0], o_halves[1]) + cross * decays[c]["cross"]

    devs = {it: outs[it] - _seg_dot(outs[it], seg) for it in items}
    for sq, c in items:
        d = devs[sq, c]
        yn = d * lax.rsqrt(_seg_dot(d * d, seg) + RET_GN_EPS)
        g = g_ref[c, sq * rows:(sq + 1) * rows, :]
        y_ref[c, sq * rows:(sq + 1) * rows, :] = g * jax.nn.sigmoid(g) * yn

    @pl.when(ci == pl.num_programs(1) - 1)
    def _():
        so_ref[...] = s_scr[...]


def _retention(p, s0_pairs, seg16, n_seq, seq_blk, rows, n_chunks, n_valid, row0):
    assert seq_blk == 1 or n_chunks == 1
    blk_rows = seq_blk * rows

    def grp(gi):
        return pl.BlockSpec((3, blk_rows, LANES), lambda b, c: (gi, row0 // blk_rows + b * n_chunks + c, 0))

    st = pl.BlockSpec((seq_blk, 3, LANES, LANES), lambda b, c: (b, 0, 0, 0))
    return pl.pallas_call(
        functools.partial(_retention_kernel, rows=rows, n_valid=n_valid),
        grid=(n_seq // seq_blk, n_chunks),
        in_specs=[grp(3), grp(4), grp(5), grp(6), st, pl.BlockSpec(seg16.shape, lambda b, c: (0, 0))],
        out_specs=[pl.BlockSpec((3, blk_rows, LANES), lambda b, c: (0, b * n_chunks + c, 0)), st],
        out_shape=[jax.ShapeDtypeStruct((3, n_seq * n_chunks * rows, LANES), f32),
                   jax.ShapeDtypeStruct((n_seq, 3, LANES, LANES), f32)],
        scratch_shapes=[pltpu.VMEM((seq_blk, 3, LANES, LANES), f32)],
        compiler_params=_cparams(("parallel", "arbitrary")),
        name="retention",
    )(p, p, p, p, s0_pairs, seg16)


def _dil_prompt_kernel(q_ref, k_ref, v_ref, y_ref, o_scr, l_scr):
    i = pl.program_id(1)
    n_steps = pl.num_programs(1)
    blk = DIL_BLK
    lo = _lane_lo((blk, LANES))
    ii = lax.broadcasted_iota(jnp.int32, (blk, 2 * blk), 0)
    jj = lax.broadcasted_iota(jnp.int32, (blk, 2 * blk), 1)
    band = jnp.logical_and(jj >= ii, jj <= ii + blk)

    for p, dil in enumerate(DILATIONS):
        nb = q_ref.shape[1] // (dil * blk)
        r = i // nb
        cb = i % nb
        start_q = r + dil * blk * cb
        start_p = r + dil * blk * jnp.maximum(cb - 1, 0)
        valid = jnp.logical_and(band, jnp.logical_or(cb > 0, jj >= blk))

        def rows(ref, c, start):
            if dil == 1:
                return ref[c, pl.ds(start, blk), :]
            return ref[c, pl.ds(start, blk, stride=dil), :]

        for c in range(3):
            q = rows(q_ref, c, start_q)
            kwin = jnp.concatenate([rows(k_ref, c, start_p), rows(k_ref, c, start_q)], axis=0).astype(bf16)
            vwin = jnp.concatenate([rows(v_ref, c, start_p), rows(v_ref, c, start_q)], axis=0).astype(bf16)
            outs, lses = [], []
            for half in range(2):
                qm = jnp.where(lo if half == 0 else jnp.logical_not(lo), q, 0.0).astype(bf16)
                s = lax.dot_general(qm, kwin, _NT, preferred_element_type=f32)
                s = jnp.where(valid, s, -jnp.inf)
                mx = jnp.max(s, axis=-1, keepdims=True)
                e = jnp.exp(s - mx)
                tot = jnp.sum(e, axis=-1, keepdims=True)
                lse = mx + jnp.log(tot)
                prob = (e / tot).astype(bf16)
                outs.append(jnp.dot(prob, vwin, preferred_element_type=f32))
                lses.append(lse)
            o_pair = jnp.where(lo, outs[0], outs[1])
            l_pair = jnp.where(lo, lses[0], lses[1])
            if dil == 1:
                o_scr[p, c, pl.ds(start_q, blk), :] = o_pair
                l_scr[p, c, pl.ds(start_q, blk), :] = l_pair
            else:
                o_scr[p, c, pl.ds(start_q, blk, stride=dil), :] = o_pair
                l_scr[p, c, pl.ds(start_q, blk, stride=dil), :] = l_pair

    @pl.when(i == n_steps - 1)
    def _():
        def merge(j, carry):
            sl = pl.ds(pl.multiple_of(j * blk, blk), blk)
            for c in range(3):
                ls = [l_scr[p, c, sl, :] for p in range(3)]
                mx = jnp.maximum(jnp.maximum(ls[0], ls[1]), ls[2])
                es = [jnp.exp(l - mx) for l in ls]
                tot = es[0] + es[1] + es[2]
                acc = (es[0] / tot) * o_scr[0, c, sl, :]
                acc = acc + (es[1] / tot) * o_scr[1, c, sl, :]
                acc = acc + (es[2] / tot) * o_scr[2, c, sl, :]
                y_ref[c, sl, :] = acc
            return carry

        lax.fori_loop(0, q_ref.shape[1] // blk, merge, 0)


def _dil_prompt(p, n_seq, t):
    def grp(gi):
        return pl.BlockSpec((3, t, LANES), lambda b, i: (gi, b, 0))

    n_steps = t // DIL_BLK
    return pl.pallas_call(
        _dil_prompt_kernel,
        grid=(n_seq, n_steps),
        in_specs=[grp(7), grp(8), grp(9)],
        out_specs=pl.BlockSpec((3, t, LANES), lambda b, i: (0, b, 0)),
        out_shape=jax.ShapeDtypeStruct((3, n_seq * t, LANES), f32),
        scratch_shapes=[pltpu.VMEM((3, 3, t, LANES), f32), pltpu.VMEM((3, 3, t, LANES), f32)],
        compiler_params=_cparams(("parallel", "arbitrary")),
        name="dil_prompt",
    )(p, p, p)


N_NEW = 4


def _dil_sample_kernel(q_ref, kn_ref, vn_ref, kc_ref, vc_ref, *rest, n_buf, aliased):
    y_ref, ko_ref, vo_ref = rest[2:] if aliased else rest
    t_all = n_buf + LANES
    zpad = jnp.zeros((LANES - DEC_PAD, LANES), f32)
    qi = lax.broadcasted_iota(jnp.int32, (DEC_PAD, t_all), 0)
    tok = lax.broadcasted_iota(jnp.int32, (DEC_PAD, t_all), 1)
    dist = n_buf + qi - tok
    masks = [jnp.logical_and(jnp.logical_and(dist >= 0, dist <= DIL_BLK * dil), (dist & (dil - 1)) == 0)
             for dil in DILATIONS]
    row_ok = lax.broadcasted_iota(jnp.int32, (DEC_PAD, LANES), 0) < N_NEW
    tail_ok = lax.broadcasted_iota(jnp.int32, (HEAD_DIM, LANES), 1) < N_NEW

    for c in range(3):
        kn_t = jnp.concatenate([kn_ref[c], zpad], axis=0).T
        vn_t = jnp.concatenate([vn_ref[c], zpad], axis=0).T
        q_pair = q_ref[c]
        y_halves = []
        for half in range(2):
            h = 2 * c + half
            k_tail = jnp.where(tail_ok, kn_t[HEAD_DIM * half:HEAD_DIM * (half + 1)], 0.0)
            v_tail = jnp.where(tail_ok, vn_t[HEAD_DIM * half:HEAD_DIM * (half + 1)], 0.0)
            k_all = jnp.concatenate([kc_ref[0, 0, h], k_tail], axis=1)
            v_all = jnp.concatenate([vc_ref[0, 0, h], v_tail], axis=1)
            ko_ref[0, 0, h] = pltpu.roll(k_all, t_all - N_NEW, 1)[:, :n_buf]
            vo_ref[0, 0, h] = pltpu.roll(v_all, t_all - N_NEW, 1)[:, :n_buf]
            q16 = q_pair[:, HEAD_DIM * half:HEAD_DIM * (half + 1)].astype(bf16)
            s = jnp.dot(q16, k_all.astype(bf16), preferred_element_type=f32)
            probs, lses = [], []
            for msk in masks:
                sm = jnp.where(msk, s, -jnp.inf)
                mx = jnp.max(sm, axis=-1, keepdims=True)
                lse = mx + jnp.log(jnp.sum(jnp.exp(sm - mx), axis=-1, keepdims=True))
                probs.append(jnp.exp(sm - lse).astype(bf16))
                lses.append(lse)
            o = lax.dot_general(jnp.concatenate(probs, axis=0), v_all.astype(bf16), _NT,
                                preferred_element_type=f32)
            mx = jnp.maximum(jnp.maximum(lses[0], lses[1]), lses[2])
            es = [jnp.exp(l - mx) for l in lses]
            tot = es[0] + es[1] + es[2]
            y_halves.append((es[0] / tot) * o[0:DEC_PAD] + (es[1] / tot) * o[DEC_PAD:2 * DEC_PAD]
                            + (es[2] / tot) * o[2 * DEC_PAD:3 * DEC_PAD])
        y_ref[c] = jnp.where(row_ok, jnp.concatenate(y_halves, axis=1), 0.0)


def _dil_sample(p, cache_k_t, cache_v_t, layer, prev_outs, n_seq, row0):
    n_buf = cache_k_t.shape[-1]

    def grp(gi):
        return pl.BlockSpec((3, DEC_PAD, LANES), lambda b: (gi, row0 // DEC_PAD + b, 0))

    buf = pl.BlockSpec((1, 1, H_DIL, HEAD_DIM, n_buf), lambda b: (layer, b, 0, 0, 0))
    aliased = prev_outs is not None
    extra = list(prev_outs) if aliased else []
    return pl.pallas_call(
        functools.partial(_dil_sample_kernel, n_buf=n_buf, aliased=aliased),
        grid=(n_seq,),
        in_specs=[grp(7), grp(8), grp(9), buf, buf] + [pl.BlockSpec(memory_space=pl.ANY)] * len(extra),
        out_specs=[pl.BlockSpec((3, DEC_PAD, LANES), lambda b: (0, b, 0)), buf, buf],
        out_shape=[jax.ShapeDtypeStruct((3, n_seq * DEC_PAD, LANES), f32),
                   jax.ShapeDtypeStruct(cache_k_t.shape, f32),
                   jax.ShapeDtypeStruct(cache_v_t.shape, f32)],
        input_output_aliases={5: 1, 6: 2} if aliased else {},
        compiler_params=_cparams(("arbitrary",)),
        name="dil_sample",
    )(p, p, p, cache_k_t, cache_v_t, *extra)


def _layer_norm(z, gamma, beta):
    mu = jnp.mean(z, axis=-1, keepdims=True)
    d = z - mu
    var = jnp.mean(d * d, axis=-1, keepdims=True)
    return d * lax.rsqrt(var + LN_EPS) * gamma + beta


def _merge_kernel(ya_ref, bon_ref, g_ref, yb_ref, yc_ref, x_ref, lnx_ref, seg_ref, wo_ref, ln_ref, o_ref):
    yr = _cat_blocks(ya_ref)
    seg = seg_ref[...]
    mean = _seg_dot(yr, seg)
    d = yr - mean
    var = _seg_dot(d * d, seg)
    ya = d * lax.rsqrt(var + RWKV_GN_EPS) * lnx_ref[0:1, :] + lnx_ref[1:2, :]
    ya = (ya + _cat_blocks(bon_ref)) * _cat_blocks(g_ref)
    ycat = jnp.concatenate([ya, _cat_blocks(yb_ref), _cat_blocks(yc_ref)], axis=1).astype(bf16)
    h = jnp.dot(ycat, wo_ref[...], preferred_element_type=f32)
    o_ref[...] = _layer_norm(DEEPNORM_ALPHA * x_ref[...] + h, ln_ref[0:1, :], ln_ref[1:2, :])


def _merge_prompt(ya_tm, bonus, g, yb, yc, x, params, n_seq, t, tm):
    nt = t // tm
    tok = pl.BlockSpec((3, tm, LANES), lambda b, ti: (0, b * nt + ti, 0))
    rows = pl.BlockSpec((tm, D_MODEL), lambda b, ti: (b * nt + ti, 0))

    def full(a):
        return pl.BlockSpec(a.shape, lambda b, ti: (0,) * a.ndim)

    return pl.pallas_call(
        _merge_kernel,
        grid=(n_seq, nt),
        in_specs=[pl.BlockSpec((3, tm, LANES), lambda b, ti: (0, ti, b)), tok, tok, tok, tok, rows]
        + [full(a) for a in params],
        out_specs=rows,
        out_shape=jax.ShapeDtypeStruct(x.shape, f32),
        compiler_params=_cparams(("parallel", "parallel")),
        name="mixer_merge_prompt",
    )(ya_tm, bonus, g, yb, yc, x, *params)


def _merge_rows(ya, bonus, g, yb, yc, x, params, tm):
    blk = pl.BlockSpec((3, tm, LANES), lambda i: (0, i, 0))
    rows = pl.BlockSpec((tm, D_MODEL), lambda i: (i, 0))

    def full(a):
        return pl.BlockSpec(a.shape, lambda i: (0,) * a.ndim)

    return pl.pallas_call(
        _merge_kernel,
        grid=(x.shape[0] // tm,),
        in_specs=[blk] * 5 + [rows] + [full(a) for a in params],
        out_specs=rows,
        out_shape=jax.ShapeDtypeStruct(x.shape, f32),
        compiler_params=_cparams(("parallel",)),
        name="mixer_merge_rows",
    )(ya, bonus, g, yb, yc, x, *params)


EXPERT_BLOCK = 16


def _first_max(x, idx, axes, sentinel):
    mx = x
    for ax in axes:
        mx = jnp.max(mx, axis=ax, keepdims=True)
    am = jnp.where(x == mx, idx, sentinel)
    for ax in axes:
        am = jnp.min(am, axis=ax, keepdims=True)
    return mx, am


def _route(logits_t, bias_col):
    n_tok = logits_t.shape[1]
    per = N_EXPERTS // N_ROUTE_GROUPS
    scores = jax.nn.sigmoid(logits_t)
    shp = (N_ROUTE_GROUPS, per, n_tok)
    s3 = scores.reshape(shp)
    b3 = (scores + bias_col).reshape(shp)
    sub = lax.broadcasted_iota(jnp.int32, shp, 1)
    grp = lax.broadcasted_iota(jnp.int32, shp, 0)
    eid = grp * per + sub
    neg = -jnp.inf

    m1, i1 = _first_max(b3, sub, (1,), per)
    m2, _ = _first_max(jnp.where(sub == i1, neg, b3), sub, (1,), per)
    gscore = m1 + m2
    gid = lax.broadcasted_iota(jnp.int32, gscore.shape, 0)
    gsel = jnp.zeros(gscore.shape, jnp.bool_)
    for _ in range(TOPK_GROUPS):
        _, gi = _first_max(gscore, gid, (0,), N_ROUTE_GROUPS)
        hit = gid == gi
        gsel = jnp.logical_or(gsel, hit)
        gscore = jnp.where(hit, neg, gscore)
    masked = jnp.where(gsel, b3, neg)
    esel = jnp.zeros(shp, jnp.bool_)
    for _ in range(TOP_K):
        _, ei = _first_max(masked, eid, (1, 0), N_EXPERTS)
        hit = eid == ei
        esel = jnp.logical_or(esel, hit)
        masked = jnp.where(hit, neg, masked)
    top_s = jnp.where(esel, s3, 0.0)
    denom = jnp.sum(jnp.sum(top_s, axis=1, keepdims=True), axis=0, keepdims=True)
    return (top_s / denom * ROUTE_SCALE).reshape(N_EXPERTS, n_tok)


def _moe_kernel(x_ref, wr_ref, rb_ref, wg_ref, wu_ref, wd_ref, wsg_ref, wsu_ref, wsd_ref, ln_ref,
                o_ref, x16_scr, comb_scr, acc_scr):
    j = pl.program_id(1)

    @pl.when(j == 0)
    def _():
        x16 = x_ref[...].astype(bf16)
        x16_scr[...] = x16
        logits_t = lax.dot_general(wr_ref[...], x16, _NT, preferred_element_type=f32)
        comb_t = _route(logits_t, rb_ref[...])
        comb_t = jnp.concatenate([comb_t, jnp.zeros_like(comb_t)], axis=0)
        comb = comb_t.T
        comb_scr[0] = comb
        for jj in range(1, N_EXPERTS // EXPERT_BLOCK):
            comb_scr[jj] = pltpu.roll(comb, LANES - EXPERT_BLOCK * jj, 1)
        sg = jnp.dot(x16, wsg_ref[...], preferred_element_type=f32)
        su = jnp.dot(x16, wsu_ref[...], preferred_element_type=f32)
        hs = (sg * jax.nn.sigmoid(sg) * su).astype(bf16)
        acc_scr[...] = jnp.dot(hs, wsd_ref[...], preferred_element_type=f32)

    x16 = x16_scr[...]
    gate = jnp.dot(x16, wg_ref[...], preferred_element_type=f32)
    up = jnp.dot(x16, wu_ref[...], preferred_element_type=f32)
    comb = comb_scr[j]
    tm = comb.shape[0]
    cexp = jnp.concatenate([jnp.broadcast_to(comb[:, e:e + 1], (tm, D_EXPERT)) for e in range(EXPERT_BLOCK)], axis=1)
    h = (gate * jax.nn.sigmoid(gate) * up * cexp).astype(bf16)
    acc_scr[...] += jnp.dot(h, wd_ref[...], preferred_element_type=f32)

    @pl.when(j == pl.num_programs(1) - 1)
    def _():
        o_ref[...] = _layer_norm(DEEPNORM_ALPHA * x_ref[...] + acc_scr[...], ln_ref[0:1, :], ln_ref[1:2, :])


def _moe(x, wr_t, rb_col, wg, wu, wd, wsg, wsu, wsd, ln, tm):
    m = x.shape[0]
    nb = EXPERT_BLOCK * D_EXPERT

    def full(a):
        return pl.BlockSpec(a.shape, lambda i, j: (0,) * a.ndim)

    return pl.pallas_call(
        _moe_kernel,
        grid=(m // tm, N_EXPERTS // EXPERT_BLOCK),
        in_specs=[pl.BlockSpec((tm, D_MODEL), lambda i, j: (i, 0)), full(wr_t), full(rb_col),
                  pl.BlockSpec((D_MODEL, nb), lambda i, j: (0, j)),
                  pl.BlockSpec((D_MODEL, nb), lambda i, j: (0, j)),
                  pl.BlockSpec((nb, D_MODEL), lambda i, j: (j, 0)),
                  full(wsg), full(wsu), full(wsd), full(ln)],
        out_specs=pl.BlockSpec((tm, D_MODEL), lambda i, j: (i, 0)),
        out_shape=jax.ShapeDtypeStruct((m, D_MODEL), f32),
        scratch_shapes=[pltpu.VMEM((tm, D_MODEL), bf16),
                        pltpu.VMEM((N_EXPERTS // EXPERT_BLOCK, tm, LANES), f32),
                        pltpu.VMEM((tm, D_MODEL), f32)],
        compiler_params=_cparams(("parallel", "arbitrary")),
        name="moe",
    )(x, wr_t, rb_col, wg, wu, wd, wsg, wsu, wsd, ln)


def _pad_heads(a, n_heads):
    return jnp.pad(a, [(0, 0)] * (a.ndim - 1) + [(0, GROUP - n_heads * HEAD_DIM)])


def _pad_w_in(w):
    widths = [C_RWKV] * 3 + [C_RET] * 4 + [C_DIL] * 3
    parts, off = [], 0
    for wd in widths:
        parts.append(jnp.pad(w[:, off:off + wd], ((0, 0), (0, GROUP - wd))))
        off += wd
    return jnp.concatenate(parts, axis=1).astype(bf16)


def _rope_tables(pos, inv_freq):
    ang = pos.astype(f32)[:, None] * inv_freq[None, :]
    cos, sin = jnp.cos(ang), jnp.sin(ang)
    return jnp.tile(jnp.concatenate([cos, cos], axis=1), (1, 2)), jnp.tile(jnp.concatenate([-sin, sin], axis=1), (1, 2))


def _segment_matrix(width, scale):
    idx = np.arange(width) // HEAD_DIM
    return jnp.asarray((idx[:, None] == idx[None, :]).astype(np.float32) * scale)


def _pairs_from_heads(s):
    b = s.shape[0]
    s6 = jnp.pad(s, ((0, 0), (0, 1), (0, 0), (0, 0))).reshape(b, 3, 2, HEAD_DIM, HEAD_DIM)
    z = jnp.zeros((b, 3, HEAD_DIM, HEAD_DIM), s.dtype)
    top = jnp.concatenate([s6[:, :, 0], z], axis=-1)
    bot = jnp.concatenate([z, s6[:, :, 1]], axis=-1)
    return jnp.concatenate([top, bot], axis=-2)


def _heads_from_pairs(sp):
    b = sp.shape[0]
    h0 = sp[:, :, :HEAD_DIM, :HEAD_DIM]
    h1 = sp[:, :, HEAD_DIM:, HEAD_DIM:]
    return jnp.stack([h0, h1], axis=2).reshape(b, 6, HEAD_DIM, HEAD_DIM)[:, :H_RET]


def _rwkv_pairs_from_heads(s):
    b = s.shape[0]
    s6 = jnp.pad(s, ((0, 0), (0, 1), (0, 0), (0, 0))).reshape(b, 3, 2, HEAD_DIM, HEAD_DIM)
    return jnp.transpose(s6, (1, 0, 3, 2, 4)).reshape(3, b, HEAD_DIM, LANES)


def _rwkv_heads_from_pairs(sp):
    b = sp.shape[1]
    s6 = jnp.transpose(sp.reshape(3, b, HEAD_DIM, 2, HEAD_DIM), (1, 0, 3, 2, 4))
    return s6.reshape(b, 6, HEAD_DIM, HEAD_DIM)[:, :H_RWKV]


def kernel(x_prompt, x_sample, state_rwkv, state_shift, state_ret, cache_k_win, cache_v_win, w_in, mu_rkv, mu_wag, w0, w1, w2, a0, a1, a2, g1, g2, k_k, k_a, r_k, lnx_g, lnx_b, w_out, ln1_g, ln1_b, w_router, router_bias, w_gate, w_up, w_down, ws_gate, ws_up, ws_down, ln2_g, ln2_b):
    bp, tp, _ = x_prompt.shape
    bs, ts, _ = x_sample.shape
    n_buf = cache_k_win.shape[2]
    depth = w_in.shape[0]
    assert ts == N_NEW and n_buf == DILATIONS[-1] * DIL_BLK and tp % (DILATIONS[-1] * DIL_BLK) == 0
    mp = bp * tp
    ms = bs * DEC_PAD
    tm, tm_moe = ROW_TILE, MOE_ROW_TILE
    assert tp % tm == 0 and ms % tm_moe == 0 and mp % tm_moe == 0 and bs % SEQ_BLOCK == 0

    inv_ret = 1.0 / (10000.0 ** jnp.linspace(0.0, 1.0, HEAD_DIM // 2, dtype=f32))
    inv_dil = ROPE_THETA ** (-jnp.arange(0, HEAD_DIM, 2, dtype=f32) / HEAD_DIM)
    pos_p = jnp.tile(jnp.arange(tp), bp)
    pos_s = jnp.tile(PAST_LEN + jnp.arange(DEC_PAD), bs)
    tabs_p = _rope_tables(pos_p, inv_ret) + _rope_tables(pos_p, inv_dil)
    tabs_s = _rope_tables(pos_s, inv_ret) + _rope_tables(pos_s, inv_dil)
    zero_tab = jnp.zeros((bs, LANES), f32)

    seg_sum = _segment_matrix(GROUP, 1.0).astype(bf16)
    seg_mean = _segment_matrix(GROUP, 1.0 / HEAD_DIM).astype(bf16)
    seg_mean_pair = _segment_matrix(LANES, 1.0 / HEAD_DIM).astype(bf16)
    seg_pair_bf16 = _segment_matrix(LANES, 1.0).astype(bf16)

    x_p = x_prompt.reshape(mp, D_MODEL)
    x_s = jnp.pad(x_sample, ((0, 0), (0, DEC_PAD - ts), (0, 0))).reshape(ms, D_MODEL)

    cache_k_t = jnp.transpose(cache_k_win, (0, 1, 3, 4, 2))
    cache_v_t = jnp.transpose(cache_v_win, (0, 1, 3, 4, 2))
    win_outs = None

    outs_p = [[] for _ in range(5)]
    outs_s = [[] for _ in range(3)]
    for l in range(depth):
        xp3 = x_p.reshape(bp, tp, D_MODEL)
        xs3 = x_s.reshape(bs, DEC_PAD, D_MODEL)
        x_last_s = state_shift[l]
        outs_p[1].append(xp3[:, -1])
        outs_s[1].append(xs3[:, ts - 1])

        w_in_p = _pad_w_in(w_in[l])
        p_p, k_new_t, v_new_t = _inproj(x_p, w_in_p, *tabs_p, tm, seq_len=tp)
        p_s, = _inproj(x_s, w_in_p, *tabs_s, tm)
        p_last, = _inproj(x_last_s, w_in_p, zero_tab, zero_tab, zero_tab, zero_tab, min(bs, 128))

        xs_prev = jnp.concatenate([x_last_s[:, None], xs3[:, :-1]], axis=1).reshape(ms, D_MODEL)
        pr_s = p_s[:9].reshape(9, bs, DEC_PAD, LANES)
        pp_s = jnp.concatenate([p_last[:9][:, :, None], pr_s[:, :, :-1]], axis=2).reshape(9, ms, LANES)

        vec = jnp.stack([_pad_heads(v, H_RWKV) for v in (w0[l], a0[l], k_k[l], k_a[l], r_k[l].reshape(-1))])
        prep_params = (_pad_heads(mu_rkv[l], H_RWKV), mu_wag[l], vec,
                       w1[l].astype(bf16), _pad_heads(w2[l], H_RWKV).astype(bf16),
                       a1[l].astype(bf16), _pad_heads(a2[l], H_RWKV).astype(bf16),
                       g1[l].astype(bf16), _pad_heads(g2[l], H_RWKV).astype(bf16), seg_sum)
        prep_p = _rwkv_prep_prompt(x_p, p_p, prep_params, bp, tp, tm)
        prep_s = _rwkv_prep_rows(x_s, xs_prev, p_s, pp_s, prep_params, 0, ms, tm)

        ops_s = [jnp.swapaxes(a.reshape(3, bs, DEC_PAD, LANES), 1, 2).reshape(3, DEC_PAD, bs * LANES)
                 for a in prep_s[:7]]
        ya_p, s_rwkv_p = _rwkv_scan(list(prep_p[:7]), jnp.zeros((3, bp, HEAD_DIM, LANES), f32), seg_pair_bf16,
                                    bp, SCAN_CHUNK, SCAN_CHUNK)
        ya_s, s_rwkv_s = _rwkv_scan(ops_s, _rwkv_pairs_from_heads(state_rwkv[l]), seg_pair_bf16,
                                    SEQ_BLOCK, DEC_PAD, ts)
        ya_s = jnp.swapaxes(ya_s.reshape(3, DEC_PAD, bs, LANES), 1, 2).reshape(3, ms, LANES)
        outs_p[0].append(_rwkv_heads_from_pairs(s_rwkv_p))
        outs_s[0].append(_rwkv_heads_from_pairs(s_rwkv_s))

        yb_p, s_ret_p = _retention(p_p, jnp.zeros((bp, 3, LANES, LANES), f32), seg_mean_pair,
                                   bp, 1, RET_CHUNK, tp // RET_CHUNK, RET_CHUNK, 0)
        yb_s, s_ret_s = _retention(p_s, _pairs_from_heads(state_ret[l]), seg_mean_pair,
                                   bs, SEQ_BLOCK, DEC_PAD, 1, ts, 0)
        outs_p[2].append(_heads_from_pairs(s_ret_p))
        outs_s[2].append(_heads_from_pairs(s_ret_s))

        yc_p = _dil_prompt(p_p, bp, tp)
        yc_s, k_win_t, v_win_t = _dil_sample(p_s, cache_k_t, cache_v_t, l, win_outs, bs, 0)
        win_outs = (k_win_t, v_win_t)
        keep = min(n_buf, tp)
        outs_p[3].append(k_new_t[..., tp - keep:])
        outs_p[4].append(v_new_t[..., tp - keep:])

        w_out_p = jnp.concatenate([
            jnp.pad(w_out[l][:C_RWKV], ((0, GROUP - C_RWKV), (0, 0))),
            jnp.pad(w_out[l][C_RWKV:C_RWKV + C_RET], ((0, GROUP - C_RET), (0, 0))),
            w_out[l][C_RWKV + C_RET:]], axis=0).astype(bf16)
        lnx = jnp.stack([_pad_heads(lnx_g[l], H_RWKV), _pad_heads(lnx_b[l], H_RWKV)])
        merge_params = (lnx, seg_mean, w_out_p, jnp.stack([ln1_g[l], ln1_b[l]]))
        x_p = _merge_prompt(ya_p, prep_p[8], prep_p[7], yb_p, yc_p, x_p, merge_params, bp, tp, tm)
        x_s = _merge_rows(ya_s, prep_s[8], prep_s[7], yb_s, yc_s, x_s, merge_params, tm)

        wg = jnp.transpose(w_gate[l], (1, 0, 2)).reshape(D_MODEL, N_EXPERTS * D_EXPERT).astype(bf16)
        wu = jnp.transpose(w_up[l], (1, 0, 2)).reshape(D_MODEL, N_EXPERTS * D_EXPERT).astype(bf16)
        wd = w_down[l].reshape(N_EXPERTS * D_EXPERT, D_MODEL).astype(bf16)
        moe_params = (w_router[l].T.astype(bf16), router_bias[l].reshape(N_EXPERTS, 1), wg, wu, wd,
                      ws_gate[l].astype(bf16), ws_up[l].astype(bf16), ws_down[l].astype(bf16),
                      jnp.stack([ln2_g[l], ln2_b[l]]))
        x_p = _moe(x_p, *moe_params, tm_moe)
        x_s = _moe(x_s, *moe_params, tm_moe)

    y_prompt = x_p.reshape(bp, tp, D_MODEL)
    y_sample = x_s.reshape(bs, DEC_PAD, D_MODEL)[:, :ts]
    wins_p = tuple(jnp.transpose(jnp.stack(o), (0, 1, 4, 2, 3)) for o in outs_p[3:])
    wins_s = tuple(jnp.transpose(w, (0, 1, 4, 2, 3)) for w in win_outs)
    return ((y_prompt, y_sample) + tuple(jnp.stack(o) for o in outs_p[:3]) + wins_p
            + tuple(jnp.stack(o) for o in outs_s) + wins_s)
```
